```python
import jax, jax.numpy as jnp
from jax import lax
import numpy as np

D_MODEL = 1024
BATCH = 2
SEQ = 8192
DEPTH = 4
DEC_BATCH = 32
DEC_SEQ = 4
PAST_LEN = 8192
PAGE_SIZE = 128

N_HEADS = 8
HEAD_DIM = 64
ATTN_W = N_HEADS * HEAD_DIM
N_IDX_HEADS = 8
IDX_DIM = 64
IDX_W_SCALE = (N_IDX_HEADS * IDX_DIM) ** -0.5
TOPK_MAX = 256
Q_BLOCK = 128
ROPE_THETA = 500000.0
CONV_CH = 512
CONV_W = 31
N_EXPERTS = 64
EXPERT_FF = 256
MOE_TOPK = 8
N_GROUPS = 8
TOPK_GROUPS = 4
ROUTED_SCALE = 2.5
MOE_BLOCK = 128
ALPHA = (2 * DEPTH) ** 0.25
BETA = (8 * DEPTH) ** -0.25
LN_EPS = 1e-5
SEG_SIZES = (ATTN_W, ATTN_W, ATTN_W, N_IDX_HEADS * IDX_DIM, IDX_DIM, N_IDX_HEADS, 2 * CONV_CH, 2 * D_MODEL)
SPLITS = [sum(SEG_SIZES[:i + 1]) for i in range(len(SEG_SIZES) - 1)]
D_IN = sum(SEG_SIZES)

kernel_name = 'hybrid_dsa_conformer_moe_step'


def layer_norm(x, g, b):
    xf = x.astype(jnp.float32)
    mu = xf.mean(-1, keepdims=True)
    var = jnp.square(xf - mu).mean(-1, keepdims=True)
    y = (xf - mu) * lax.rsqrt(var + LN_EPS) * g.astype(jnp.float32) + b.astype(jnp.float32)
    return y.astype(x.dtype)


def rope(x, pos):
    rd = x.shape[-1] // 4
    half = rd // 2
    inv = ROPE_THETA ** (-jnp.arange(half, dtype=jnp.float32) * 2.0 / rd)
    ang = pos.astype(jnp.float32)[:, None] * inv[None, :]
    cos = jnp.cos(ang)[:, None, :].astype(x.dtype)
    sin = jnp.sin(ang)[:, None, :].astype(x.dtype)
    x1, x2, rest = x[..., :half], x[..., half:rd], x[..., rd:]
    return jnp.concatenate([x1 * cos - x2 * sin, x2 * cos + x1 * sin, rest], axis=-1)


def mixer_inputs(x, pos, p):
    B, T, _ = x.shape
    h = x @ p['w_in'] + p['b_in']
    q, k, v, qi, ki, wi, glu, gates = jnp.split(h, SPLITS, axis=-1)
    q = rope(q.reshape(B, T, N_HEADS, HEAD_DIM), pos)
    k = rope(k.reshape(B, T, N_HEADS, HEAD_DIM), pos)
    v = v.reshape(B, T, N_HEADS, HEAD_DIM)
    qi = rope(qi.reshape(B, T, N_IDX_HEADS, IDX_DIM), pos)
    ki = rope(layer_norm(ki, p['idx_k_ln_g'], p['idx_k_ln_b'])[:, :, None, :], pos)[:, :, 0, :]
    wi = wi * IDX_W_SCALE
    a, b = jnp.split(glu, 2, axis=-1)
    u = a * jax.nn.sigmoid(b)
    g_attn, g_conv = jnp.split(jax.nn.sigmoid(gates), 2, axis=-1)
    return q, k, v, qi, ki, wi, u, g_attn, g_conv


def indexer_scores(qi, wi, ki):
    dots = jnp.einsum('bthd,bsd->bths', qi, ki)
    return jnp.einsum('bths,bth->bts', jax.nn.relu(dots), wi).astype(jnp.float32)


def select_keys(scores, q_pos, n_keys, topk):
    k_pos = jnp.arange(n_keys)
    admissible = k_pos[None, None, :] <= q_pos[None, :, None]
    _, idx = lax.top_k(jnp.where(admissible, scores, -jnp.inf), topk)
    valid = idx <= q_pos[None, :, None]
    return idx, valid


def sparse_attend(q, kg, vg, valid):
    s = jnp.einsum('bthd,btkhd->bthk', q, kg).astype(jnp.float32) * (HEAD_DIM ** -0.5)
    s = jnp.where(valid[:, :, None, :], s, -jnp.inf)
    pr = jax.nn.softmax(s, axis=-1).astype(vg.dtype)
    return jnp.einsum('bthk,btkhd->bthd', pr, vg)


def prompt_attention(q, k, v, qi, ki, wi):
    B, S = q.shape[:2]
    topk = min(TOPK_MAX, S // 4)
    bidx = jnp.arange(B)[:, None, None]

    def block(j):
        t0 = j * Q_BLOCK
        qb = lax.dynamic_slice_in_dim(q, t0, Q_BLOCK, axis=1)
        qib = lax.dynamic_slice_in_dim(qi, t0, Q_BLOCK, axis=1)
        wib = lax.dynamic_slice_in_dim(wi, t0, Q_BLOCK, axis=1)
        q_pos = t0 + jnp.arange(Q_BLOCK)
        idx, valid = select_keys(indexer_scores(qib, wib, ki), q_pos, S, topk)
        return sparse_attend(qb, k[bidx, idx], v[bidx, idx], valid)

    out = lax.map(block, jnp.arange(S // Q_BLOCK))
    return out.transpose(1, 0, 2, 3, 4).reshape(B, S, ATTN_W)


def sample_attention(q, k_new, v_new, qi, ki_new, wi, cache_k, cache_v, cache_idx_k, page_table, layer):
    Bd, Tn = q.shape[:2]
    past = page_table.shape[1] * PAGE_SIZE
    L = past + Tn
    topk = min(TOPK_MAX, L // 4)
    ki_past = cache_idx_k[layer, page_table].reshape(Bd, past, IDX_DIM).astype(ki_new.dtype)
    ki_all = jnp.concatenate([ki_past, ki_new], axis=1)
    q_pos = past + jnp.arange(Tn)
    idx, valid = select_keys(indexer_scores(qi, wi, ki_all), q_pos, L, topk)
    bidx = jnp.arange(Bd)[:, None, None]
    in_past = (idx < past)[..., None, None]
    pidx = jnp.minimum(idx, past - 1)
    phys = page_table[bidx, pidx // PAGE_SIZE]
    off = pidx % PAGE_SIZE
    nidx = jnp.clip(idx - past, 0, Tn - 1)
    kg = jnp.where(in_past, cache_k[layer, phys, off].astype(k_new.dtype), k_new[bidx, nidx])
    vg = jnp.where(in_past, cache_v[layer, phys, off].astype(v_new.dtype), v_new[bidx, nidx])
    return sparse_attend(q, kg, vg, valid).reshape(Bd, Tn, ATTN_W)


def depthwise_causal(u_ext, w, b):
    out = lax.conv_general_dilated(u_ext, w[:, None, :].astype(u_ext.dtype), window_strides=(1,),
                                   padding='VALID', dimension_numbers=('NWC', 'WIO', 'NWC'),
                                   feature_group_count=CONV_CH)
    return out + b


def route(xf, w_router, router_bias):
    T = xf.shape[0]
    scores = jax.nn.sigmoid((xf @ w_router).astype(jnp.float32))
    sel = scores + router_bias.astype(jnp.float32)
    grp = sel.reshape(T, N_GROUPS, N_EXPERTS // N_GROUPS)
    grp_score = lax.top_k(grp, 2)[0].sum(-1)
    _, gidx = lax.top_k(grp_score, TOPK_GROUPS)
    gmask = jax.nn.one_hot(gidx, N_GROUPS, dtype=jnp.float32).sum(1) > 0
    emask = jnp.repeat(gmask, N_EXPERTS // N_GROUPS, axis=1)
    _, eidx = lax.top_k(jnp.where(emask, sel, -jnp.inf), MOE_TOPK)
    w = jnp.take_along_axis(scores, eidx, axis=1)
    w = w / w.sum(-1, keepdims=True) * ROUTED_SCALE
    return eidx, w


def routed_experts(xf, eidx, gw, w_g, w_u, w_d):
    T, D = xf.shape
    A = T * MOE_TOPK
    flat_e = eidx.reshape(A)
    order = jnp.argsort(flat_e)
    e_sorted = flat_e[order]
    tok_sorted = (order // MOE_TOPK).astype(jnp.int32)
    w_sorted = gw.reshape(A)[order]
    counts = jnp.bincount(flat_e, length=N_EXPERTS)
    padded = (counts + MOE_BLOCK - 1) // MOE_BLOCK * MOE_BLOCK
    pad_end = jnp.cumsum(padded)
    start = jnp.cumsum(counts) - counts
    dest = (pad_end - padded)[e_sorted] + jnp.arange(A) - start[e_sorted]
    n_blocks = -(-A // MOE_BLOCK) + N_EXPERTS
    P = n_blocks * MOE_BLOCK
    slot_tok = jnp.full((P,), T, jnp.int32).at[dest].set(tok_sorted)
    slot_w = jnp.zeros((P,), jnp.float32).at[dest].set(w_sorted)
    blk_e = jnp.minimum(jnp.searchsorted(pad_end, jnp.arange(n_blocks) * MOE_BLOCK, side='right'),
                        N_EXPERTS - 1)
    x_pad = jnp.concatenate([xf, jnp.zeros((1, D), xf.dtype)], axis=0)

    def run(args):
        tok, e = args
        xb = x_pad[tok]
        h = jax.nn.silu(xb @ w_g[e]) * (xb @ w_u[e])
        return h @ w_d[e]

    out = lax.map(run, (slot_tok.reshape(n_blocks, MOE_BLOCK), blk_e))
    out = out.reshape(P, D).astype(jnp.float32) * slot_w[:, None]
    y = jnp.zeros((T + 1, D), jnp.float32).at[slot_tok].add(out)[:T]
    return y.astype(xf.dtype)


def moe(x, p):
    B, T, D = x.shape
    xf = x.reshape(B * T, D)
    eidx, gw = route(xf, p['w_router'], p['router_bias'])
    shared = (jax.nn.silu(xf @ p['w_sh_gate']) * (xf @ p['w_sh_up'])) @ p['w_sh_down']
    routed = routed_experts(xf, eidx, gw, p['w_exp_gate'], p['w_exp_up'], p['w_exp_down'])
    return (routed + shared).reshape(B, T, D)


def merge_and_channel_mix(x, attn, conv_pre, g_attn, g_conv, p):
    conv_out = jax.nn.silu(layer_norm(conv_pre, p['conv_ln_g'], p['conv_ln_b'])) @ p['w_pc']
    mix = (g_attn * (attn @ p['w_pa']) + g_conv * conv_out) @ p['w_o']
    x = layer_norm(ALPHA * x + mix, p['ln1_g'], p['ln1_b'])
    return layer_norm(ALPHA * x + moe(x, p), p['ln2_g'], p['ln2_b'])


def setup_inputs(seed: int = 0) -> dict:
    key = jax.random.key(seed)
    ks = jax.random.split(key, 32)
    f32 = jnp.float32
    n_pages = PAST_LEN // PAGE_SIZE
    n_used = DEC_BATCH * n_pages
    n_pool = n_used + max(1, n_used // 4)

    def nrm(k, shape, scale):
        return jax.random.normal(k, shape, f32) * scale

    col_scale = jnp.ones((D_IN,), f32).at[SPLITS[1]:SPLITS[2]].set(BETA)
    return {
        'x_prompt': nrm(ks[0], (BATCH, SEQ, D_MODEL), 1.0),
        'x_sample': nrm(ks[1], (DEC_BATCH, DEC_SEQ, D_MODEL), 1.0),
        'cache_k': nrm(ks[2], (DEPTH, n_pool, PAGE_SIZE, N_HEADS, HEAD_DIM), 1.0),
        'cache_v': nrm(ks[3], (DEPTH, n_pool, PAGE_SIZE, N_HEADS, HEAD_DIM), BETA),
        'cache_idx_k': nrm(ks[4], (DEPTH, n_pool, PAGE_SIZE, IDX_DIM), 1.0),
        'state_conv': nrm(ks[5], (DEPTH, DEC_BATCH, CONV_W - 1, CONV_CH), 0.5),
        'page_table': jax.random.permutation(ks[6], n_pool)[:n_used].reshape(DEC_BATCH, n_pages).astype(jnp.int32),
        'w_in': nrm(ks[7], (DEPTH, D_MODEL, D_IN), D_MODEL ** -0.5) * col_scale,
        'b_in': nrm(ks[8], (DEPTH, D_IN), 0.02),
        'idx_k_ln_g': 1.0 + nrm(ks[9], (DEPTH, IDX_DIM), 0.02),
        'idx_k_ln_b': nrm(ks[10], (DEPTH, IDX_DIM), 0.02),
        'conv_w': nrm(ks[11], (DEPTH, CONV_W, CONV_CH), CONV_W ** -0.5),
        'conv_b': nrm(ks[12], (DEPTH, CONV_CH), 0.02),
        'conv_ln_g': 1.0 + nrm(ks[13], (DEPTH, CONV_CH), 0.02),
        'conv_ln_b': nrm(ks[14], (DEPTH, CONV_CH), 0.02),
        'w_pa': nrm(ks[15], (DEPTH, ATTN_W, D_MODEL), BETA * ATTN_W ** -0.5),
        'w_pc': nrm(ks[16], (DEPTH, CONV_CH, D_MODEL), BETA * CONV_CH ** -0.5),
        'w_o': nrm(ks[17], (DEPTH, D_MODEL, D_MODEL), BETA * D_MODEL ** -0.5),
        'ln1_g': 1.0 + nrm(ks[18], (DEPTH, D_MODEL), 0.02),
        'ln1_b': nrm(ks[19], (DEPTH, D_MODEL), 0.02),
        'w_router': nrm(ks[20], (DEPTH, D_MODEL, N_EXPERTS), D_MODEL ** -0.5),
        'router_bias': nrm(ks[21], (DEPTH, N_EXPERTS), 0.01),
        'w_exp_gate': nrm(ks[22], (DEPTH, N_EXPERTS, D_MODEL, EXPERT_FF), D_MODEL ** -0.5),
        'w_exp_up': nrm(ks[23], (DEPTH, N_EXPERTS, D_MODEL, EXPERT_FF), D_MODEL ** -0.5),
        'w_exp_down': nrm(ks[24], (DEPTH, N_EXPERTS, EXPERT_FF, D_MODEL), BETA * EXPERT_FF ** -0.5),
        'w_sh_gate': nrm(ks[25], (DEPTH, D_MODEL, EXPERT_FF), D_MODEL ** -0.5),
        'w_sh_up': nrm(ks[26], (DEPTH, D_MODEL, EXPERT_FF), D_MODEL ** -0.5),
        'w_sh_down': nrm(ks[27], (DEPTH, EXPERT_FF, D_MODEL), BETA * EXPERT_FF ** -0.5),
        'ln2_g': 1.0 + nrm(ks[28], (DEPTH, D_MODEL), 0.02),
        'ln2_b': nrm(ks[29], (DEPTH, D_MODEL), 0.02),
    }


def reference(x_prompt, x_sample, cache_k, cache_v, cache_idx_k, state_conv, page_table,
              w_in, b_in, idx_k_ln_g, idx_k_ln_b, conv_w, conv_b, conv_ln_g, conv_ln_b,
              w_pa, w_pc, w_o, ln1_g, ln1_b, w_router, router_bias,
              w_exp_gate, w_exp_up, w_exp_down, w_sh_gate, w_sh_up, w_sh_down, ln2_g, ln2_b):
    pos_p = jnp.arange(x_prompt.shape[1])
    pos_s = page_table.shape[1] * PAGE_SIZE + jnp.arange(x_sample.shape[1])
    xp, xs = x_prompt, x_sample
    nk_p, nv_p, nik_p, nc_p = [], [], [], []
    nk_s, nv_s, nik_s, nc_s = [], [], [], []
    for l in range(DEPTH):
        p = {'w_in': w_in[l], 'b_in': b_in[l], 'idx_k_ln_g': idx_k_ln_g[l], 'idx_k_ln_b': idx_k_ln_b[l],
             'conv_ln_g': conv_ln_g[l], 'conv_ln_b': conv_ln_b[l], 'w_pa': w_pa[l], 'w_pc': w_pc[l],
             'w_o': w_o[l], 'ln1_g': ln1_g[l], 'ln1_b': ln1_b[l], 'w_router': w_router[l],
             'router_bias': router_bias[l], 'w_exp_gate': w_exp_gate[l], 'w_exp_up': w_exp_up[l],
             'w_exp_down': w_exp_down[l], 'w_sh_gate': w_sh_gate[l], 'w_sh_up': w_sh_up[l],
             'w_sh_down': w_sh_down[l], 'ln2_g': ln2_g[l], 'ln2_b': ln2_b[l]}
        q, k, v, qi, ki, wi, u, ga, gc = mixer_inputs(xp, pos_p, p)
        attn = prompt_attention(q, k, v, qi, ki, wi)
        conv = depthwise_causal(jnp.pad(u, ((0, 0), (CONV_W - 1, 0), (0, 0))), conv_w[l], conv_b[l])
        nk_p.append(k)
        nv_p.append(v)
        nik_p.append(ki)
        nc_p.append(u[:, u.shape[1] - (CONV_W - 1):])
        xp = merge_and_channel_mix(xp, attn, conv, ga, gc, p)
        q, k, v, qi, ki, wi, u, ga, gc = mixer_inputs(xs, pos_s, p)
        attn = sample_attention(q, k, v, qi, ki, wi, cache_k, cache_v, cache_idx_k, page_table, l)
        u_ext = jnp.concatenate([state_conv[l].astype(u.dtype), u], axis=1)
        conv = depthwise_causal(u_ext, conv_w[l], conv_b[l])
        nk_s.append(k)
        nv_s.append(v)
        nik_s.append(ki)
        nc_s.append(u_ext[:, u_ext.shape[1] - (CONV_W - 1):])
        xs = merge_and_channel_mix(xs, attn, conv, ga, gc, p)
    return (xp, xs, jnp.stack(nk_p), jnp.stack(nv_p), jnp.stack(nik_p), jnp.stack(nc_p),
            jnp.stack(nk_s), jnp.stack(nv_s), jnp.stack(nik_s), jnp.stack(nc_s))
```

```python
import functools
import math

import jax
import jax.numpy as jnp
from jax import lax
from jax.experimental import pallas as pl
from jax.experimental.pallas import tpu as pltpu

F32 = jnp.float32
BF16 = jnp.bfloat16
I32 = jnp.int32

N_HEADS = 8
HEAD_DIM = 64
ATTN_W = N_HEADS * HEAD_DIM
N_IDX_HEADS = 8
IDX_DIM = 64
IDX_W_SCALE = (N_IDX_HEADS * IDX_DIM) ** -0.5
TOPK_MAX = 256
ROPE_THETA = 500000.0
CONV_CH = 512
CONV_W = 31
N_EXPERTS = 64
EXPERT_FF = 256
MOE_TOPK = 8
N_GROUPS = 8
GROUP_SIZE = N_EXPERTS // N_GROUPS
TOPK_GROUPS = 4
ROUTED_SCALE = 2.5
LN_EPS = 1e-5

LANES = 128
Q_ROWS = 128
KEY_CHUNK = 512
INT_MIN = -(2 ** 31)
NEG_BIG = -1e30
VMEM_LIMIT = 56 * 1024 * 1024


def _cparams(*sem):
    return pltpu.CompilerParams(dimension_semantics=sem, vmem_limit_bytes=VMEM_LIMIT)


def _layer_norm(x, g, b):
    mu = jnp.mean(x, axis=-1, keepdims=True)
    xc = x - mu
    var = jnp.mean(xc * xc, axis=-1, keepdims=True)
    return xc * lax.rsqrt(var + LN_EPS) * g + b


def _dot(a, b):
    return jnp.dot(a, b, preferred_element_type=F32)


def _dot_nt(a, b):
    return lax.dot_general(a, b, (((1,), (1,)), ((), ())), preferred_element_type=F32)


def _inproj_kernel(x_ref, wa_ref, ws_ref, wg_ref, wt_ref, ba_ref, bs_ref, bg_ref, bt_ref,
                   lng_ref, lnb_ref, cos_ref, s1_ref, s2_ref,
                   q_ref, kf_ref, kb_ref, vf_ref, vb_ref, qi_ref, small_ref, kib_ref,
                   u_ref, ga_ref, gc_ref):
    xb = x_ref[...].astype(BF16)
    cos = cos_ref[...]
    s1 = s1_ref[...]
    s2 = s2_ref[...]

    def rope128(v, c, a, b):
        return v * c + pltpu.roll(v, LANES - 8, 1) * a + pltpu.roll(v, 8, 1) * b

    def rope_seg(seg):
        return jnp.concatenate(
            [rope128(seg[:, c * LANES:(c + 1) * LANES], cos, s1, s2) for c in range(ATTN_W // LANES)], axis=1)

    ha = _dot(xb, wa_ref[...]) + ba_ref[...]
    q = rope_seg(ha[:, 0:ATTN_W])
    k = rope_seg(ha[:, ATTN_W:2 * ATTN_W])
    v = ha[:, 2 * ATTN_W:3 * ATTN_W]
    qi = rope_seg(ha[:, 3 * ATTN_W:4 * ATTN_W])
    q_ref[...] = (q * (HEAD_DIM ** -0.5)).astype(BF16)
    kf_ref[...] = k
    kb_ref[...] = k.astype(BF16)
    vf_ref[...] = v
    vb_ref[...] = v.astype(BF16)
    qi_ref[...] = qi.astype(BF16)

    hs = _dot(xb, ws_ref[...]) + bs_ref[...]
    lane = lax.broadcasted_iota(I32, hs.shape, 1)
    is_key = lane < IDX_DIM
    mu = jnp.sum(jnp.where(is_key, hs, 0.0), axis=1, keepdims=True) * (1.0 / IDX_DIM)
    hc = jnp.where(is_key, hs - mu, 0.0)
    var = jnp.sum(hc * hc, axis=1, keepdims=True) * (1.0 / IDX_DIM)
    kin = hc * lax.rsqrt(var + LN_EPS) * lng_ref[...] + lnb_ref[...]
    kir = rope128(kin, jnp.where(is_key, cos, 1.0), jnp.where(is_key, s1, 0.0), jnp.where(is_key, s2, 0.0))
    small = jnp.where(is_key, kir, jnp.where(lane < IDX_DIM + N_IDX_HEADS, hs * IDX_W_SCALE, 0.0))
    small_ref[...] = small
    kib_ref[...] = small[:, 0:IDX_DIM].astype(BF16)

    hg = _dot(xb, wg_ref[...]) + bg_ref[...]
    u_ref[...] = hg[:, 0:CONV_CH] * jax.nn.sigmoid(hg[:, CONV_CH:2 * CONV_CH])

    d = ga_ref.shape[1]
    ht = jax.nn.sigmoid(_dot(xb, wt_ref[...]) + bt_ref[...])
    ga_ref[...] = ht[:, 0:d]
    gc_ref[...] = ht[:, d:2 * d]


def _inproj(x, lw, tables, tm):
    t, d = x.shape
    cos, s1, s2 = tables
    full = lambda a: pl.BlockSpec(a.shape, lambda i: (0,) * a.ndim)
    rows = lambda w: pl.BlockSpec((tm, w), lambda i: (i, 0))
    weights = (lw['w_a'], lw['w_s'], lw['w_g'], lw['w_t'], lw['b_a'], lw['b_s'], lw['b_g'], lw['b_t'],
               lw['idx_ln_g'], lw['idx_ln_b'])
    out_shapes = (
        jax.ShapeDtypeStruct((t, ATTN_W), BF16),
        jax.ShapeDtypeStruct((t, ATTN_W), F32),
        jax.ShapeDtypeStruct((t, ATTN_W), BF16),
        jax.ShapeDtypeStruct((t, ATTN_W), F32),
        jax.ShapeDtypeStruct((t, ATTN_W), BF16),
        jax.ShapeDtypeStruct((t, ATTN_W), BF16),
        jax.ShapeDtypeStruct((t, LANES), F32),
        jax.ShapeDtypeStruct((t, IDX_DIM), BF16),
        jax.ShapeDtypeStruct((t, CONV_CH), F32),
        jax.ShapeDtypeStruct((t, d), F32),
        jax.ShapeDtypeStruct((t, d), F32),
    )
    return pl.pallas_call(
        _inproj_kernel,
        grid=(t // tm,),
        in_specs=[rows(d)] + [full(w) for w in weights] + [rows(LANES)] * 3,
        out_specs=tuple(rows(s.shape[1]) for s in out_shapes),
        out_shape=out_shapes,
        compiler_params=_cparams("parallel"),
        name="inproj",
    )(x, *weights, cos, s1, s2)


def _sort_key(score, admissible):
    bits = pltpu.bitcast(score, I32)
    bits = jnp.where(score == 0.0, 0, bits)
    key = jnp.where(bits >= 0, bits, bits ^ 0x7FFFFFFF)
    return jnp.where(admissible, key, INT_MIN)


def _fold_lanes(x):
    w = x.shape[1]
    acc = x[:, 0:LANES]
    for c in range(1, w // LANES):
        acc = acc + x[:, c * LANES:(c + 1) * LANES]
    return acc


def _select_threshold(keys_ref, rows, chunk, n_chunks, ksel, n_cols):
    lane = lax.broadcasted_iota(I32, (rows, chunk), 1)

    def count(pred):
        def body(c, cnt):
            off = pl.multiple_of(c * chunk, chunk)
            kc = keys_ref[:, pl.ds(off, chunk)]
            return cnt + _fold_lanes(jnp.where(pred(kc, off), 1, 0))
        cnt = lax.fori_loop(0, n_chunks, body, jnp.zeros((rows, LANES), I32))
        return jnp.sum(cnt.astype(F32), axis=1, keepdims=True).astype(I32)

    cnt0 = count(lambda kc, off: kc >= 0)
    thr0 = jnp.where(cnt0 >= ksel, 0, INT_MIN).astype(I32)

    def bit_body(i, thr):
        cand = thr | lax.shift_left(jnp.int32(1), 30 - i)
        cnt = count(lambda kc, off: kc >= cand)
        return jnp.where(cnt >= ksel, cand, thr)

    thr = lax.fori_loop(0, 31, bit_body, thr0)

    cnt_gt = count(lambda kc, off: kc > thr)
    cnt_ge = count(lambda kc, off: kc >= thr)
    need = ksel - cnt_gt
    live = thr != INT_MIN
    over = jnp.logical_and(live, cnt_ge > ksel)
    n_bits = max(1, (n_cols - 1).bit_length())

    def tie_search():
        def body(i, m):
            cand = m | lax.shift_left(jnp.int32(1), n_bits - 1 - i)
            g = count(lambda kc, off: jnp.logical_and(kc == thr, (off + lane) < cand))
            return jnp.where(g < need, cand, m)
        return lax.fori_loop(0, n_bits, body, jnp.zeros((rows, 1), I32))

    any_over = jnp.max(jnp.where(over, 1.0, 0.0)) > 0.5
    m = lax.cond(any_over, tie_search, lambda: jnp.zeros((rows, 1), I32))
    cut = jnp.where(live, jnp.where(over, m, n_cols), -1).astype(I32)
    return thr, cut


def _selected(kc, cols, thr, cut):
    return jnp.logical_or(kc > thr, jnp.logical_and(kc == thr, cols <= cut))


def _prompt_attn_kernel(qi_ref, small_ref, q_ref, ki_ref, k_ref, v_ref, o_ref, keys_ref, acc_ref,
                        *, seq, chunk, ksel):
    j = pl.program_id(1)
    n_chunks = (j * Q_ROWS + Q_ROWS + chunk - 1) // chunk
    qi = qi_ref[0]
    wi = small_ref[0]
    row = j * Q_ROWS + lax.broadcasted_iota(I32, (Q_ROWS, chunk), 0)
    lane = lax.broadcasted_iota(I32, (Q_ROWS, chunk), 1)
    wcols = [wi[:, IDX_DIM + h:IDX_DIM + h + 1] for h in range(N_IDX_HEADS)]

    def idx_body(c, carry):
        off = pl.multiple_of(c * chunk, chunk)
        kic = ki_ref[0, pl.ds(off, chunk), :]
        score = jnp.zeros((Q_ROWS, chunk), F32)
        for h in range(N_IDX_HEADS):
            d = _dot_nt(qi[:, h * IDX_DIM:(h + 1) * IDX_DIM], kic)
            score = score + jnp.maximum(d, 0.0) * wcols[h]
        keys_ref[:, pl.ds(off, chunk)] = _sort_key(score, (off + lane) <= row)
        return carry

    lax.fori_loop(0, n_chunks, idx_body, 0)
    thr, cut = _select_threshold(keys_ref, Q_ROWS, chunk, n_chunks, ksel, seq)

    q = q_ref[0]
    acc_ref[...] = jnp.zeros_like(acc_ref)

    def attn_body(c, carry):
        ms, ls = carry
        off = pl.multiple_of(c * chunk, chunk)
        kc = keys_ref[:, pl.ds(off, chunk)]
        bias = jnp.where(_selected(kc, off + lane, thr, cut), 0.0, NEG_BIG)
        kk = k_ref[0, pl.ds(off, chunk), :]
        vv = v_ref[0, pl.ds(off, chunk), :]
        new_ms, new_ls = [], []
        for h in range(N_HEADS):
            hs = slice(h * HEAD_DIM, (h + 1) * HEAD_DIM)
            s = _dot_nt(q[:, hs], kk[:, hs]) + bias
            m_new = jnp.maximum(ms[h], jnp.max(s, axis=1, keepdims=True))
            alpha = jnp.exp(ms[h] - m_new)
            p = jnp.exp(s - m_new)
            new_ls.append(alpha * ls[h] + jnp.sum(p, axis=1, keepdims=True))
            new_ms.append(m_new)
            acc_ref[:, hs] = alpha * acc_ref[:, hs] + _dot(p.astype(BF16), vv[:, hs])
        return tuple(new_ms), tuple(new_ls)

    init = (tuple(jnp.full((Q_ROWS, 1), NEG_BIG, F32) for _ in range(N_HEADS)),
            tuple(jnp.zeros((Q_ROWS, 1), F32) for _ in range(N_HEADS)))
    _, ls = lax.fori_loop(0, n_chunks, attn_body, init)
    for h in range(N_HEADS):
        hs = slice(h * HEAD_DIM, (h + 1) * HEAD_DIM)
        o_ref[0, :, hs] = (acc_ref[:, hs] / ls[h]).astype(o_ref.dtype)


def _prompt_attn(qi, small, q, kib, kb, vb):
    b, s, _ = q.shape
    chunk = min(KEY_CHUNK, s)
    ksel = min(TOPK_MAX, s // 4)
    qblk = lambda w: pl.BlockSpec((1, Q_ROWS, w), lambda bi, j: (bi, j, 0))
    whole = lambda w: pl.BlockSpec((1, s, w), lambda bi, j: (bi, 0, 0))
    return pl.pallas_call(
        functools.partial(_prompt_attn_kernel, seq=s, chunk=chunk, ksel=ksel),
        grid=(b, s // Q_ROWS),
        in_specs=[qblk(ATTN_W), qblk(LANES), qblk(ATTN_W), whole(IDX_DIM), whole(ATTN_W), whole(ATTN_W)],
        out_specs=qblk(ATTN_W),
        out_shape=jax.ShapeDtypeStruct((b, s, ATTN_W), BF16),
        scratch_shapes=[pltpu.VMEM((Q_ROWS, s), I32), pltpu.VMEM((Q_ROWS, ATTN_W), F32)],
        compiler_params=_cparams("parallel", "arbitrary"),
        name="prompt_attn",
    )(qi, small, q, kib, kb, vb)


SAMPLE_ROWS = 8
PAGES_PER_STEP = 8
HEAD_ROWS = N_HEADS * SAMPLE_ROWS


def _sample_select_kernel(pt_ref, qi_ref, wcol_ref, kin_ref, *rest, page, past, n_new, ksel):
    pages = rest[:PAGES_PER_STEP]
    bias_ref = rest[PAGES_PER_STEP]
    keys_ref = rest[PAGES_PER_STEP + 1]
    st = pl.program_id(1)
    n_steps = pl.num_programs(1)
    qi = qi_ref[0]
    wcol = wcol_ref[0]
    n_cols = past + page
    lane = lax.broadcasted_iota(I32, (SAMPLE_ROWS, page), 1)
    row = lax.broadcasted_iota(I32, (SAMPLE_ROWS, page), 0)

    def scores(ki_bf16):
        d = jnp.maximum(_dot_nt(qi, ki_bf16), 0.0) * wcol
        sc = d[0:SAMPLE_ROWS]
        for h in range(1, N_IDX_HEADS):
            sc = sc + d[h * SAMPLE_ROWS:(h + 1) * SAMPLE_ROWS]
        return sc

    for p in range(PAGES_PER_STEP):
        off = pl.multiple_of((st * PAGES_PER_STEP + p) * page, page)
        keys_ref[:, pl.ds(off, page)] = _sort_key(scores(pages[p][0, 0].astype(BF16)), row < n_new)

    @pl.when(st == n_steps - 1)
    def _():
        adm = jnp.logical_and(row < n_new, lane <= row)
        keys_ref[:, pl.ds(past, page)] = _sort_key(scores(kin_ref[0]), adm)
        thr, cut = _select_threshold(keys_ref, SAMPLE_ROWS, page, n_cols // page, ksel, n_cols)
        for c in range(n_cols // page):
            kc = keys_ref[:, c * page:(c + 1) * page]
            bias_ref[0, :, c * page:(c + 1) * page] = jnp.where(
                _selected(kc, c * page + lane, thr, cut), 0.0, NEG_BIG)


def _sample_attn_kernel(pt_ref, q_ref, bias_ref, knew_ref, vnew_ref, *rest, page, past):
    kpages = rest[:PAGES_PER_STEP]
    vpages = rest[PAGES_PER_STEP:2 * PAGES_PER_STEP]
    o_ref = rest[2 * PAGES_PER_STEP]
    m_ref, l_ref, acc_ref = rest[2 * PAGES_PER_STEP + 1:]
    st = pl.program_id(1)
    n_steps = pl.num_programs(1)
    q = q_ref[0]

    @pl.when(st == 0)
    def _():
        m_ref[...] = jnp.full_like(m_ref, NEG_BIG)
        l_ref[...] = jnp.zeros_like(l_ref)
        acc_ref[...] = jnp.zeros_like(acc_ref)

    def step(kpage, vpage, bias_tok):
        bias = jnp.concatenate([bias_tok] * N_HEADS, axis=0)
        s = _dot_nt(q, kpage) + bias
        m_old = m_ref[...]
        m_new = jnp.maximum(m_old, jnp.max(s, axis=1, keepdims=True))
        alpha = jnp.exp(m_old - m_new)
        p = jnp.exp(s - m_new)
        l_ref[...] = alpha * l_ref[...] + jnp.sum(p, axis=1, keepdims=True)
        m_ref[...] = m_new
        acc_ref[...] = alpha * acc_ref[...] + _dot(p.astype(BF16), vpage)

    for p in range(PAGES_PER_STEP):
        off = pl.multiple_of((st * PAGES_PER_STEP + p) * page, page)
        step(kpages[p][0, 0].astype(BF16), vpages[p][0, 0].astype(BF16), bias_ref[0, :, pl.ds(off, page)])

    @pl.when(st == n_steps - 1)
    def _():
        step(knew_ref[0], vnew_ref[0], bias_ref[0, :, past:past + page])
        out = acc_ref[...] / l_ref[...]
        for h in range(N_HEADS):
            hs = slice(h * HEAD_DIM, (h + 1) * HEAD_DIM)
            o_ref[0, :, hs] = out[h * SAMPLE_ROWS:(h + 1) * SAMPLE_ROWS, hs]


def _sample_attention(layer, page_table, q, qi, small, kib_new, kb_new, vb_new, cache_k, cache_v, cache_idx_k):
    bd, n_new, _ = q.shape
    n_pages = page_table.shape[1]
    page = cache_k.shape[2]
    past = n_pages * page
    ksel = min(TOPK_MAX, (past + n_new) // 4)
    n_steps = n_pages // PAGES_PER_STEP
    n_cols = past + page

    pad_tok = lambda a: jnp.pad(a, ((0, 0), (0, SAMPLE_ROWS - n_new), (0, 0)))
    head_major = lambda a: pad_tok(a).reshape(bd, SAMPLE_ROWS, N_HEADS, -1).transpose(0, 2, 1, 3)
    qi_rows = head_major(qi).reshape(bd, HEAD_ROWS, IDX_DIM)
    wcol = head_major(small[:, :, IDX_DIM:IDX_DIM + N_IDX_HEADS]).reshape(bd, HEAD_ROWS, 1)
    qh = head_major(q)
    eye = jnp.eye(N_HEADS, dtype=q.dtype)
    q_rows = (qh[:, :, :, None, :] * eye[None, :, None, :, None]).reshape(bd, HEAD_ROWS, ATTN_W)
    pad_rows = lambda a: jnp.pad(a, ((0, 0), (0, page - n_new), (0, 0)))
    kin_pad, knew_pad, vnew_pad = pad_rows(kib_new), pad_rows(kb_new), pad_rows(vb_new)

    def page_spec(width, p):
        return pl.BlockSpec((1, 1, page, width),
                            lambda b, st, pt, p=p: (layer, pt[b, st * PAGES_PER_STEP + p], 0, 0))

    per_b = lambda shape: pl.BlockSpec((1,) + shape, lambda b, st, pt: (b, 0, 0))

    bias = pl.pallas_call(
        functools.partial(_sample_select_kernel, page=page, past=past, n_new=n_new, ksel=ksel),
        grid_spec=pltpu.PrefetchScalarGridSpec(
            num_scalar_prefetch=1,
            grid=(bd, n_steps),
            in_specs=[per_b((HEAD_ROWS, IDX_DIM)), per_b((HEAD_ROWS, 1)), per_b((page, IDX_DIM))]
                     + [page_spec(IDX_DIM, p) for p in range(PAGES_PER_STEP)],
            out_specs=per_b((SAMPLE_ROWS, n_cols)),
            scratch_shapes=[pltpu.VMEM((SAMPLE_ROWS, n_cols), I32)],
        ),
        out_shape=jax.ShapeDtypeStruct((bd, SAMPLE_ROWS, n_cols), F32),
        compiler_params=_cparams("parallel", "arbitrary"),
        name="sample_select",
    )(page_table, qi_rows, wcol, kin_pad, *([cache_idx_k] * PAGES_PER_STEP))

    out = pl.pallas_call(
        functools.partial(_sample_attn_kernel, page=page, past=past),
        grid_spec=pltpu.PrefetchScalarGridSpec(
            num_scalar_prefetch=1,
            grid=(bd, n_steps),
            in_specs=[per_b((HEAD_ROWS, ATTN_W)), per_b((SAMPLE_ROWS, n_cols)), per_b((page, ATTN_W)),
                      per_b((page, ATTN_W))] + [page_spec(ATTN_W, p) for p in range(PAGES_PER_STEP)] * 2,
            out_specs=per_b((SAMPLE_ROWS, ATTN_W)),
            scratch_shapes=[pltpu.VMEM((HEAD_ROWS, 1), F32), pltpu.VMEM((HEAD_ROWS, 1), F32),
                            pltpu.VMEM((HEAD_ROWS, ATTN_W), F32)],
        ),
        out_shape=jax.ShapeDtypeStruct((bd, SAMPLE_ROWS, ATTN_W), F32),
        compiler_params=_cparams("parallel", "arbitrary"),
        name="sample_attn",
    )(page_table, q_rows, bias, knew_pad, vnew_pad, *([cache_k] * PAGES_PER_STEP), *([cache_v] * PAGES_PER_STEP))
    return out[:, :n_new]


CONV_HALO = 32


def _conv_prompt_kernel(u_ref, prev_ref, w_ref, b_ref, o_ref, ext_ref, *, tile):
    i = pl.program_id(1)
    halo = prev_ref[0, tile - CONV_HALO:tile, :]
    ext_ref[0:CONV_HALO, :] = jnp.where(i > 0, halo, 0.0)
    ext_ref[CONV_HALO:CONV_HALO + tile, :] = u_ref[0]
    acc = jnp.broadcast_to(b_ref[...], (tile, CONV_CH))
    base = CONV_HALO - (CONV_W - 1)
    for k in range(CONV_W):
        acc = acc + ext_ref[base + k:base + k + tile, :] * w_ref[k:k + 1, :]
    o_ref[0] = acc


def _conv_prompt(u, w, b, tile):
    bsz, s, c = u.shape
    return pl.pallas_call(
        functools.partial(_conv_prompt_kernel, tile=tile),
        grid=(bsz, s // tile),
        in_specs=[pl.BlockSpec((1, tile, c), lambda bi, i: (bi, i, 0)),
                  pl.BlockSpec((1, tile, c), lambda bi, i: (bi, jnp.maximum(i - 1, 0), 0)),
                  pl.BlockSpec(w.shape, lambda bi, i: (0, 0)),
                  pl.BlockSpec(b.shape, lambda bi, i: (0, 0))],
        out_specs=pl.BlockSpec((1, tile, c), lambda bi, i: (bi, i, 0)),
        out_shape=jax.ShapeDtypeStruct((bsz, s, c), F32),
        scratch_shapes=[pltpu.VMEM((CONV_HALO + tile, c), F32)],
        compiler_params=_cparams("parallel", "arbitrary"),
        name="conv_prompt",
    )(u, u, w, b)


def _conv_sample_kernel(state_ref, u_ref, w_ref, b_ref, o_ref):
    n_hist = state_ref.shape[0]
    n_new = u_ref.shape[0]
    for t in range(n_new):
        acc = jnp.broadcast_to(b_ref[...], o_ref.shape[1:])
        for j in range(t, n_hist):
            acc = acc + state_ref[j] * w_ref[j - t:j - t + 1, :]
        for i in range(t + 1):
            k = n_hist - t + i
            acc = acc + u_ref[i] * w_ref[k:k + 1, :]
        o_ref[t] = acc


def _conv_sample(state_t, u_t, w, b):
    return pl.pallas_call(
        _conv_sample_kernel,
        out_shape=jax.ShapeDtypeStruct(u_t.shape, F32),
        name="conv_sample",
    )(state_t, u_t, w, b)


def _merge_kernel(x_ref, attn_ref, conv_ref, ga_ref, gc_ref, wpa_ref, wpc_ref, wo_ref,
                  cg_ref, cb_ref, g1_ref, b1_ref, o_ref, *, alpha):
    c = _layer_norm(conv_ref[...], cg_ref[...], cb_ref[...])
    c = c * jax.nn.sigmoid(c)
    conv_out = _dot(c.astype(BF16), wpc_ref[...])
    attn_out = _dot(attn_ref[...].astype(BF16), wpa_ref[...])
    mix = ga_ref[...] * attn_out + gc_ref[...] * conv_out
    mix = _dot(mix.astype(BF16), wo_ref[...])
    o_ref[...] = _layer_norm(alpha * x_ref[...] + mix, g1_ref[...], b1_ref[...])


def _merge(x, attn, conv, ga, gc, lw, alpha, tm):
    t, d = x.shape
    rows = lambda a: pl.BlockSpec((tm, a.shape[1]), lambda i: (i, 0))
    full = lambda a: pl.BlockSpec(a.shape, lambda i: (0,) * a.ndim)
    weights = (lw['w_pa'], lw['w_pc'], lw['w_o'], lw['conv_ln_g'], lw['conv_ln_b'], lw['ln1_g'], lw['ln1_b'])
    acts = (x, attn, conv, ga, gc)
    return pl.pallas_call(
        functools.partial(_merge_kernel, alpha=alpha),
        grid=(t // tm,),
        in_specs=[rows(a) for a in acts] + [full(w) for w in weights],
        out_specs=pl.BlockSpec((tm, d), lambda i: (i, 0)),
        out_shape=jax.ShapeDtypeStruct((t, d), F32),
        compiler_params=_cparams("parallel"),
        name="merge",
    )(*acts, *weights)


EXPERTS_PER_STEP = 2


def _route(x, wr_hi_ref, wr_lo_ref, rbias_ref):
    tm = x.shape[0]
    x_hi = x.astype(BF16)
    x_lo = (x - x_hi.astype(F32)).astype(BF16)
    logits = _dot_nt(wr_hi_ref[...], x_hi) + (_dot_nt(wr_hi_ref[...], x_lo) + _dot_nt(wr_lo_ref[...], x_hi))
    scores = jax.nn.sigmoid(logits)
    sel = scores + rbias_ref[...]
    neg_inf = -jnp.inf
    sub8 = lax.broadcasted_iota(I32, (GROUP_SIZE, tm), 0).astype(F32)

    def take_first_max(cur, idx, n):
        m = jnp.max(cur, axis=0, keepdims=True)
        first = jnp.min(jnp.where(cur == m, idx, float(n)), axis=0, keepdims=True)
        return m, idx == first

    gs_rows = []
    for g in range(N_GROUPS):
        blk = sel[g * GROUP_SIZE:(g + 1) * GROUP_SIZE, :]
        m1, hit = take_first_max(blk, sub8, GROUP_SIZE)
        m2 = jnp.max(jnp.where(hit, neg_inf, blk), axis=0, keepdims=True)
        gs_rows.append(m1 + m2)
    gs = jnp.concatenate(gs_rows, axis=0)
    gsel = jnp.zeros((N_GROUPS, tm), jnp.bool_)
    for _ in range(TOPK_GROUPS):
        _, hit = take_first_max(gs, sub8, N_GROUPS)
        gsel = jnp.logical_or(gsel, hit)
        gs = jnp.where(hit, neg_inf, gs)
    gself = jnp.where(gsel, 1.0, 0.0)
    emask = jnp.concatenate(
        [jnp.broadcast_to(gself[g:g + 1, :], (GROUP_SIZE, tm)) for g in range(N_GROUPS)], axis=0) > 0.5
    cur = jnp.where(emask, sel, neg_inf)
    sub64 = lax.broadcasted_iota(I32, (N_EXPERTS, tm), 0).astype(F32)
    chosen = jnp.zeros((N_EXPERTS, tm), jnp.bool_)
    for _ in range(MOE_TOPK):
        _, hit = take_first_max(cur, sub64, N_EXPERTS)
        chosen = jnp.logical_or(chosen, hit)
        cur = jnp.where(hit, neg_inf, cur)
    w = jnp.where(chosen, scores, 0.0)
    w = w / jnp.sum(w, axis=0, keepdims=True) * ROUTED_SCALE
    return w.T


def _moe_kernel(x_ref, wrh_ref, wrl_ref, rb_ref, sg_ref, su_ref, sd_ref, eg_ref, eu_ref, ed_ref,
                g2_ref, b2_ref, o_ref, xb_ref, gate_ref, acc_ref, *, alpha):
    e = pl.program_id(1)

    def ffn(xb, wg, wu):
        hg = _dot(xb, wg)
        return hg * jax.nn.sigmoid(hg) * _dot(xb, wu)

    @pl.when(e == 0)
    def _():
        x = x_ref[...]
        xb = x.astype(BF16)
        xb_ref[...] = xb
        gate_ref[...] = _route(x, wrh_ref, wrl_ref, rb_ref)
        acc_ref[...] = _dot(ffn(xb, sg_ref[...], su_ref[...]).astype(BF16), sd_ref[...])

    xb = xb_ref[...]
    gate = gate_ref[...]
    lane = lax.broadcasted_iota(I32, gate.shape, 1)
    for i in range(EXPERTS_PER_STEP):
        gcol = jnp.sum(jnp.where(lane == e * EXPERTS_PER_STEP + i, gate, 0.0), axis=1, keepdims=True)
        h = ffn(xb, eg_ref[i], eu_ref[i]) * gcol
        acc_ref[...] += _dot(h.astype(BF16), ed_ref[i])

    @pl.when(e == pl.num_programs(1) - 1)
    def _():
        o_ref[...] = _layer_norm(alpha * x_ref[...] + acc_ref[...], g2_ref[...], b2_ref[...])


def _moe(x, lw, alpha, tm):
    t, d = x.shape
    full = lambda a: pl.BlockSpec(a.shape, lambda i, e: (0,) * a.ndim)
    exp_spec = lambda a: pl.BlockSpec((EXPERTS_PER_STEP,) + a.shape[1:], lambda i, e: (e, 0, 0))
    small_w = (lw['wr_hi'], lw['wr_lo'], lw['r_bias'], lw['w_sg'], lw['w_su'], lw['w_sd'])
    exp_w = (lw['w_eg'], lw['w_eu'], lw['w_ed'])
    ln_w = (lw['ln2_g'], lw['ln2_b'])
    return pl.pallas_call(
        functools.partial(_moe_kernel, alpha=alpha),
        grid=(t // tm, N_EXPERTS // EXPERTS_PER_STEP),
        in_specs=[pl.BlockSpec((tm, d), lambda i, e: (i, 0))] + [full(w) for w in small_w]
                 + [exp_spec(w) for w in exp_w] + [full(w) for w in ln_w],
        out_specs=pl.BlockSpec((tm, d), lambda i, e: (i, 0)),
        out_shape=jax.ShapeDtypeStruct((t, d), F32),
        scratch_shapes=[pltpu.VMEM((tm, d), BF16), pltpu.VMEM((tm, N_EXPERTS), F32), pltpu.VMEM((tm, d), F32)],
        compiler_params=_cparams("parallel", "arbitrary"),
        name="moe",
    )(x, *small_w, *exp_w, *ln_w)


def _rope_tables(pos):
    rd = HEAD_DIM // 4
    half = rd // 2
    inv = ROPE_THETA ** (-jnp.arange(half, dtype=F32) * 2.0 / rd)
    ang = pos.astype(F32)[:, None] * inv[None, :]
    cos, sin = jnp.cos(ang), jnp.sin(ang)
    t = pos.shape[0]
    ones = jnp.ones((t, HEAD_DIM - rd), F32)
    zeros = jnp.zeros((t, HEAD_DIM - rd), F32)
    zh = jnp.zeros((t, half), F32)
    c64 = jnp.concatenate([cos, cos, ones], axis=1)
    a64 = jnp.concatenate([-sin, zh, zeros], axis=1)
    b64 = jnp.concatenate([zh, sin, zeros], axis=1)
    rep = lambda a: jnp.tile(a, (1, LANES // HEAD_DIM))
    return rep(c64), rep(a64), rep(b64)


def _split_hi_lo(w):
    hi = w.astype(BF16)
    return hi, (w - hi.astype(F32)).astype(BF16)


def _row_tile(t, want):
    return want if t % want == 0 else t


def kernel(x_prompt, x_sample, cache_k, cache_v, cache_idx_k, state_conv, page_table, w_in, b_in, idx_k_ln_g, idx_k_ln_b, conv_w, conv_b, conv_ln_g, conv_ln_b, w_pa, w_pc, w_o, ln1_g, ln1_b, w_router, router_bias, w_exp_gate, w_exp_up, w_exp_down, w_sh_gate, w_sh_up, w_sh_down, ln2_g, ln2_b):
    bsz, seq, d = x_prompt.shape
    bd, n_new, _ = x_sample.shape
    depth = w_in.shape[0]
    n_pool, page = cache_k.shape[1], cache_k.shape[2]
    past = page_table.shape[1] * page
    alpha = (2 * depth) ** 0.25
    assert seq % KEY_CHUNK == 0 or seq < KEY_CHUNK
    assert page == LANES and page_table.shape[1] % PAGES_PER_STEP == 0 and n_new <= SAMPLE_ROWS

    c_a = 4 * ATTN_W
    c_s = c_a + IDX_DIM + N_IDX_HEADS
    c_g = c_s + 2 * CONV_CH
    pad_s = LANES - (IDX_DIM + N_IDX_HEADS)
    row2 = lambda a: a.reshape(1, -1)

    cache_k2 = cache_k.reshape(depth, n_pool, page, ATTN_W)
    cache_v2 = cache_v.reshape(depth, n_pool, page, ATTN_W)
    tables_p = _rope_tables(jnp.arange(seq))
    tables_p = tuple(jnp.tile(a, (bsz, 1)) for a in tables_p)
    tables_s = _rope_tables(past + jnp.arange(n_new))
    tables_s = tuple(jnp.tile(a, (bd, 1)) for a in tables_s)
    state_t = state_conv.transpose(0, 2, 1, 3)

    xp = x_prompt.reshape(bsz * seq, d)
    xs = x_sample.reshape(bd * n_new, d)
    outs = {name: [] for name in ('kp', 'vp', 'kip', 'cp', 'ks', 'vs', 'kis', 'cs')}
    tm_p = _row_tile(bsz * seq, 256)
    tm_s = bd * n_new
    tm_moe = _row_tile(bsz * seq, 1024)

    for l in range(depth):
        wr_hi, wr_lo = _split_hi_lo(w_router[l].T)
        lw = {
            'w_a': w_in[l, :, :c_a].astype(BF16),
            'w_s': jnp.pad(w_in[l, :, c_a:c_s], ((0, 0), (0, pad_s))).astype(BF16),
            'w_g': w_in[l, :, c_s:c_g].astype(BF16),
            'w_t': w_in[l, :, c_g:].astype(BF16),
            'b_a': row2(b_in[l, :c_a]),
            'b_s': row2(jnp.pad(b_in[l, c_a:c_s], (0, pad_s))),
            'b_g': row2(b_in[l, c_s:c_g]),
            'b_t': row2(b_in[l, c_g:]),
            'idx_ln_g': row2(jnp.pad(idx_k_ln_g[l], (0, LANES - IDX_DIM))),
            'idx_ln_b': row2(jnp.pad(idx_k_ln_b[l], (0, LANES - IDX_DIM))),
            'w_pa': w_pa[l].astype(BF16), 'w_pc': w_pc[l].astype(BF16), 'w_o': w_o[l].astype(BF16),
            'conv_ln_g': row2(conv_ln_g[l]), 'conv_ln_b': row2(conv_ln_b[l]),
            'ln1_g': row2(ln1_g[l]), 'ln1_b': row2(ln1_b[l]),
            'wr_hi': wr_hi, 'wr_lo': wr_lo, 'r_bias': router_bias[l].reshape(-1, 1),
            'w_sg': w_sh_gate[l].astype(BF16), 'w_su': w_sh_up[l].astype(BF16), 'w_sd': w_sh_down[l].astype(BF16),
            'w_eg': w_exp_gate[l].astype(BF16), 'w_eu': w_exp_up[l].astype(BF16), 'w_ed': w_exp_down[l].astype(BF16),
            'ln2_g': row2(ln2_g[l]), 'ln2_b': row2(ln2_b[l]),
        }
        cw = jnp.pad(conv_w[l], ((0, 1), (0, 0)))
        cb = row2(conv_b[l])

        q, kf, kb, vf, vb, qi, small, kib, u, ga, gc = _inproj(xp, lw, tables_p, tm_p)
        r3 = lambda a: a.reshape(bsz, seq, a.shape[-1])
        attn = _prompt_attn(r3(qi), r3(small), r3(q), r3(kib), r3(kb), r3(vb))
        u3 = r3(u)
        conv = _conv_prompt(u3, cw, cb, min(512, seq))
        outs['kp'].append(kf.reshape(bsz, seq, N_HEADS, HEAD_DIM))
        outs['vp'].append(vf.reshape(bsz, seq, N_HEADS, HEAD_DIM))
        outs['kip'].append(r3(small)[:, :, :IDX_DIM])
        outs['cp'].append(u3[:, seq - (CONV_W - 1):])
        x1 = _merge(xp, attn.reshape(bsz * seq, ATTN_W), conv.reshape(bsz * seq, CONV_CH), ga, gc, lw, alpha, tm_p)
        xp = _moe(x1, lw, alpha, tm_moe)

        q, kf, kb, vf, vb, qi, small, kib, u, ga, gc = _inproj(xs, lw, tables_s, tm_s)
        s3 = lambda a: a.reshape(bd, n_new, a.shape[-1])
        attn = _sample_attention(l, page_table, s3(q), s3(qi), s3(small), s3(kib), s3(kb), s3(vb),
                                 cache_k2, cache_v2, cache_idx_k)
        u3 = s3(u)
        conv = _conv_sample(state_t[l], u3.transpose(1, 0, 2), cw, cb).transpose(1, 0, 2)
        outs['ks'].append(kf.reshape(bd, n_new, N_HEADS, HEAD_DIM))
        outs['vs'].append(vf.reshape(bd, n_new, N_HEADS, HEAD_DIM))
        outs['kis'].append(s3(small)[:, :, :IDX_DIM])
        outs['cs'].append(jnp.concatenate([state_conv[l][:, n_new:], u3], axis=1))
        x1 = _merge(xs, attn.reshape(bd * n_new, ATTN_W), conv.reshape(bd * n_new, CONV_CH), ga, gc, lw, alpha, tm_s)
        xs = _moe(x1, lw, alpha, tm_s)

    st = lambda name: jnp.stack(outs[name])
    return (xp.reshape(bsz, seq, d), xs.reshape(bd, n_new, d), st('kp'), st('vp'), st('kip'), st('cp'),
            st('ks'), st('vs'), st('kis'), st('cs'))
```

```python
import functools
import math

import jax
import jax.numpy as jnp
from jax import lax
from jax.experimental import pallas as pl
from jax.experimental.pallas import tpu as pltpu

F32 = jnp.float32
BF16 = jnp.bfloat16
I32 = jnp.int32

N_HEADS = 8
HEAD_DIM = 64
ATTN_W = N_HEADS * HEAD_DIM
N_IDX_HEADS = 8
IDX_DIM = 64
IDX_W_SCALE = (N_IDX_HEADS * IDX_DIM) ** -0.5
TOPK_MAX = 256
ROPE_THETA = 500000.0
CONV_CH = 512
CONV_W = 31
N_EXPERTS = 64
EXPERT_FF = 256
MOE_TOPK = 8
N_GROUPS = 8
GROUP_SIZE = N_EXPERTS // N_GROUPS
TOPK_GROUPS = 4
ROUTED_SCALE = 2.5
LN_EPS = 1e-5

LANES = 128
Q_ROWS = 128
KEY_CHUNK = 512
ATTN_CHUNK = 256
INT_MIN = -(2 ** 31)
NEG_BIG = -1e30
VMEM_LIMIT = 56 * 1024 * 1024


def _cparams(*sem):
    return pltpu.CompilerParams(dimension_semantics=sem, vmem_limit_bytes=VMEM_LIMIT)


def _layer_norm(x, g, b):
    mu = jnp.mean(x, axis=-1, keepdims=True)
    xc = x - mu
    var = jnp.mean(xc * xc, axis=-1, keepdims=True)
    return xc * lax.rsqrt(var + LN_EPS) * g + b


def _dot(a, b):
    return jnp.dot(a, b, preferred_element_type=F32)


def _dot_nt(a, b):
    return lax.dot_general(a, b, (((1,), (1,)), ((), ())), preferred_element_type=F32)


def _inproj_kernel(x_ref, wa_ref, ws_ref, wg_ref, wt_ref, ba_ref, bs_ref, bg_ref, bt_ref,
                   lng_ref, lnb_ref, cos_ref, s1_ref, s2_ref,
                   q_ref, kf_ref, kb_ref, vf_ref, vb_ref, qi_ref, small_ref, kib_ref,
                   u_ref, ga_ref, gc_ref, kt_ref, kit_ref):
    xb = x_ref[...].astype(BF16)
    cos = cos_ref[...]
    s1 = s1_ref[...]
    s2 = s2_ref[...]

    def rope128(v, c, a, b):
        return v * c + pltpu.roll(v, LANES - 8, 1) * a + pltpu.roll(v, 8, 1) * b

    def rope_seg(seg):
        return jnp.concatenate(
            [rope128(seg[:, c * LANES:(c + 1) * LANES], cos, s1, s2) for c in range(ATTN_W // LANES)], axis=1)

    ha = _dot(xb, wa_ref[...]) + ba_ref[...]
    q = rope_seg(ha[:, 0:ATTN_W])
    k = rope_seg(ha[:, ATTN_W:2 * ATTN_W])
    v = ha[:, 2 * ATTN_W:3 * ATTN_W]
    qi = rope_seg(ha[:, 3 * ATTN_W:4 * ATTN_W])
    q_ref[...] = (q * (HEAD_DIM ** -0.5)).astype(BF16)
    kf_ref[...] = k
    kb_ref[...] = k.astype(BF16)
    kt_ref[...] = k.T.astype(BF16)
    vf_ref[...] = v
    vb_ref[...] = v.astype(BF16)
    qi_ref[...] = qi.astype(BF16)

    hs = _dot(xb, ws_ref[...]) + bs_ref[...]
    lane = lax.broadcasted_iota(I32, hs.shape, 1)
    is_key = lane < IDX_DIM
    mu = jnp.sum(jnp.where(is_key, hs, 0.0), axis=1, keepdims=True) * (1.0 / IDX_DIM)
    hc = jnp.where(is_key, hs - mu, 0.0)
    var = jnp.sum(hc * hc, axis=1, keepdims=True) * (1.0 / IDX_DIM)
    kin = hc * lax.rsqrt(var + LN_EPS) * lng_ref[...] + lnb_ref[...]
    kir = rope128(kin, jnp.where(is_key, cos, 1.0), jnp.where(is_key, s1, 0.0), jnp.where(is_key, s2, 0.0))
    small = jnp.where(is_key, kir, jnp.where(lane < IDX_DIM + N_IDX_HEADS, hs * IDX_W_SCALE, 0.0))
    small_ref[...] = small
    kib_ref[...] = small[:, 0:IDX_DIM].astype(BF16)
    kit = small.T[0:IDX_DIM, :].astype(BF16)
    kit_ref[...] = jnp.concatenate([kit, kit], axis=0)

    hg = _dot(xb, wg_ref[...]) + bg_ref[...]
    u_ref[...] = hg[:, 0:CONV_CH] * jax.nn.sigmoid(hg[:, CONV_CH:2 * CONV_CH])

    d = ga_ref.shape[1]
    ht = jax.nn.sigmoid(_dot(xb, wt_ref[...]) + bt_ref[...])
    ga_ref[...] = ht[:, 0:d]
    gc_ref[...] = ht[:, d:2 * d]


def _inproj(x, lw, tables, tm):
    t, d = x.shape
    cos, s1, s2 = tables
    full = lambda a: pl.BlockSpec(a.shape, lambda i: (0,) * a.ndim)
    rows = lambda w: pl.BlockSpec((tm, w), lambda i: (i, 0))
    weights = (lw['w_a'], lw['w_s'], lw['w_g'], lw['w_t'], lw['b_a'], lw['b_s'], lw['b_g'], lw['b_t'],
               lw['idx_ln_g'], lw['idx_ln_b'])
    out_shapes = (
        jax.ShapeDtypeStruct((t, ATTN_W), BF16),
        jax.ShapeDtypeStruct((t, ATTN_W), F32),
        jax.ShapeDtypeStruct((t, ATTN_W), BF16),
        jax.ShapeDtypeStruct((t, ATTN_W), F32),
        jax.ShapeDtypeStruct((t, ATTN_W), BF16),
        jax.ShapeDtypeStruct((t, ATTN_W), BF16),
        jax.ShapeDtypeStruct((t, LANES), F32),
        jax.ShapeDtypeStruct((t, IDX_DIM), BF16),
        jax.ShapeDtypeStruct((t, CONV_CH), F32),
        jax.ShapeDtypeStruct((t, d), F32),
        jax.ShapeDtypeStruct((t, d), F32),
    )
    cols_shapes = (
        jax.ShapeDtypeStruct((ATTN_W, t), BF16),
        jax.ShapeDtypeStruct((LANES, t), BF16),
    )
    cols = lambda h: pl.BlockSpec((h, tm), lambda i: (0, i))
    return pl.pallas_call(
        _inproj_kernel,
        grid=(t // tm,),
        in_specs=[rows(d)] + [full(w) for w in weights] + [rows(LANES)] * 3,
        out_specs=tuple(rows(s.shape[1]) for s in out_shapes) + tuple(cols(s.shape[0]) for s in cols_shapes),
        out_shape=out_shapes + cols_shapes,
        compiler_params=_cparams("parallel"),
        name="inproj",
    )(x, *weights, cos, s1, s2)


def _sort_key(score, admissible):
    bits = pltpu.bitcast(score, I32)
    bits = jnp.where(score == 0.0, 0, bits)
    key = jnp.where(bits >= 0, bits, bits ^ 0x7FFFFFFF)
    return jnp.where(admissible, key, INT_MIN)


def _fold_lanes(x):
    w = x.shape[1]
    acc = x[:, 0:LANES]
    for c in range(1, w // LANES):
        acc = acc + x[:, c * LANES:(c + 1) * LANES]
    return acc


def _select_threshold(keys_ref, rows, chunk, n_chunks, ksel, n_cols):
    lane = lax.broadcasted_iota(I32, (rows, chunk), 1)

    def count(pred):
        def body(c, cnt):
            off = pl.multiple_of(c * chunk, chunk)
            kc = keys_ref[:, pl.ds(off, chunk)]
            return cnt + _fold_lanes(jnp.where(pred(kc, off), 1, 0))
        cnt = lax.fori_loop(0, n_chunks, body, jnp.zeros((rows, LANES), I32))
        return jnp.sum(cnt.astype(F32), axis=1, keepdims=True).astype(I32)

    cnt0 = count(lambda kc, off: kc >= 0)
    thr0 = jnp.where(cnt0 >= ksel, 0, INT_MIN).astype(I32)

    def bit_body(i, thr):
        cand = thr | lax.shift_left(jnp.int32(1), 30 - i)
        cnt = count(lambda kc, off: kc >= cand)
        return jnp.where(cnt >= ksel, cand, thr)

    thr = lax.fori_loop(0, 31, bit_body, thr0)

    cnt_gt = count(lambda kc, off: kc > thr)
    cnt_ge = count(lambda kc, off: kc >= thr)
    need = ksel - cnt_gt
    live = thr != INT_MIN
    over = jnp.logical_and(live, cnt_ge > ksel)
    n_bits = max(1, (n_cols - 1).bit_length())

    def tie_search():
        def body(i, m):
            cand = m | lax.shift_left(jnp.int32(1), n_bits - 1 - i)
            g = count(lambda kc, off: jnp.logical_and(kc == thr, (off + lane) < cand))
            return jnp.where(g < need, cand, m)
        return lax.fori_loop(0, n_bits, body, jnp.zeros((rows, 1), I32))

    any_over = jnp.max(jnp.where(over, 1.0, 0.0)) > 0.5
    m = lax.cond(any_over, tie_search, lambda: jnp.zeros((rows, 1), I32))
    cut = jnp.where(live, jnp.where(over, m, n_cols), -1).astype(I32)
    return thr, cut


def _selected(kc, cols, thr, cut):
    return jnp.logical_or(kc > thr, jnp.logical_and(kc == thr, cols <= cut))


def _head_masked(x):
    lane = lax.broadcasted_iota(I32, (x.shape[0], LANES), 1)
    out = []
    for h in range(N_HEADS):
        pair = x[:, (h // 2) * LANES:(h // 2 + 1) * LANES]
        keep = (lane < HEAD_DIM) if h % 2 == 0 else (lane >= HEAD_DIM)
        out.append(jnp.where(keep, pair, jnp.zeros_like(pair)))
    return out


def _prompt_attn_kernel(qi_ref, small_ref, q_ref, kit_ref, kt_ref, v_ref, o_ref, keys_ref, m_ref, l_ref, acc_ref,
                        *, seq, chunk, achunk, ksel):
    j = pl.program_id(1)
    n_chunks = (j * Q_ROWS + Q_ROWS + chunk - 1) // chunk
    wi = small_ref[0]
    row = j * Q_ROWS + lax.broadcasted_iota(I32, (Q_ROWS, chunk), 0)
    lane = lax.broadcasted_iota(I32, (Q_ROWS, chunk), 1)
    wcols = [wi[:, IDX_DIM + h:IDX_DIM + h + 1] for h in range(N_IDX_HEADS)]
    qi_h = _head_masked(qi_ref[0])

    def idx_body(c, carry):
        off = pl.multiple_of(c * chunk, chunk)
        kic = kit_ref[:, pl.ds(off, chunk)]
        score = jnp.zeros((Q_ROWS, chunk), F32)
        for h in range(N_IDX_HEADS):
            score = score + jnp.maximum(_dot(qi_h[h], kic), 0.0) * wcols[h]
        keys_ref[:, pl.ds(off, chunk)] = _sort_key(score, (off + lane) <= row)
        return carry

    lax.fori_loop(0, n_chunks, idx_body, 0)
    thr, cut = _select_threshold(keys_ref, Q_ROWS, chunk, n_chunks, ksel, seq)

    q_h = _head_masked(q_ref[0])
    n_ach = (j * Q_ROWS + Q_ROWS + achunk - 1) // achunk
    lane_a = lax.broadcasted_iota(I32, (Q_ROWS, achunk), 1)
    even = lax.broadcasted_iota(I32, (Q_ROWS, LANES), 1) < HEAD_DIM
    ones = jnp.ones((achunk, LANES), BF16)
    m_ref[...] = jnp.full(m_ref.shape, NEG_BIG, F32)
    l_ref[...] = jnp.zeros_like(l_ref)
    acc_ref[...] = jnp.zeros_like(acc_ref)

    def attn_body(c, carry):
        off = pl.multiple_of(c * achunk, achunk)
        kc = keys_ref[:, pl.ds(off, achunk)]
        bias = jnp.where(_selected(kc, off + lane_a, thr, cut), 0.0, NEG_BIG)
        for pr in range(N_HEADS // 2):
            sl = slice(pr * LANES, (pr + 1) * LANES)
            kk = kt_ref[sl, pl.ds(off, achunk)]
            vv = jnp.concatenate([v_ref[0, pl.ds(off, achunk), sl], ones], axis=1)
            pvs, alphas = [], []
            for h in (2 * pr, 2 * pr + 1):
                hl = slice(h * LANES, (h + 1) * LANES)
                s = _dot(q_h[h], kk) + bias
                m_old = m_ref[:, hl]
                m_new = jnp.maximum(m_old, jnp.max(s, axis=1, keepdims=True))
                m_ref[:, hl] = m_new
                alpha = jnp.exp(m_old - m_new)
                p = jnp.exp(s - jnp.concatenate([m_new] * (achunk // LANES), axis=1))
                pv = _dot(p.astype(BF16), vv)
                l_ref[:, hl] = alpha * l_ref[:, hl] + pv[:, LANES:]
                pvs.append(pv[:, :LANES])
                alphas.append(alpha)
            acc_ref[:, sl] = (jnp.where(even, alphas[0], alphas[1]) * acc_ref[:, sl]
                              + jnp.where(even, pvs[0], pvs[1]))
        return carry

    lax.fori_loop(0, n_ach, attn_body, 0)
    for pr in range(N_HEADS // 2):
        sl = slice(pr * LANES, (pr + 1) * LANES)
        l_pair = jnp.where(even, l_ref[:, 2 * pr * LANES:(2 * pr + 1) * LANES],
                           l_ref[:, (2 * pr + 1) * LANES:(2 * pr + 2) * LANES])
        o_ref[0, :, sl] = (acc_ref[:, sl] / l_pair).astype(o_ref.dtype)


def _prompt_attn(qi, small, q, kit, kt, vb):
    b, s, _ = q.shape
    chunk = min(KEY_CHUNK, s)
    achunk = min(ATTN_CHUNK, s)
    ksel = min(TOPK_MAX, s // 4)
    qblk = lambda w: pl.BlockSpec((1, Q_ROWS, w), lambda bi, j: (bi, j, 0))
    return pl.pallas_call(
        functools.partial(_prompt_attn_kernel, seq=s, chunk=chunk, achunk=achunk, ksel=ksel),
        grid=(b, s // Q_ROWS),
        in_specs=[qblk(ATTN_W), qblk(LANES), qblk(ATTN_W),
                  pl.BlockSpec((LANES, s), lambda bi, j: (0, bi)),
                  pl.BlockSpec((ATTN_W, s), lambda bi, j: (0, bi)),
                  pl.BlockSpec((1, s, ATTN_W), lambda bi, j: (bi, 0, 0))],
        out_specs=qblk(ATTN_W),
        out_shape=jax.ShapeDtypeStruct((b, s, ATTN_W), BF16),
        scratch_shapes=[pltpu.VMEM((Q_ROWS, s), I32), pltpu.VMEM((Q_ROWS, N_HEADS * LANES), F32),
                        pltpu.VMEM((Q_ROWS, N_HEADS * LANES), F32), pltpu.VMEM((Q_ROWS, ATTN_W), F32)],
        compiler_params=_cparams("parallel", "arbitrary"),
        name="prompt_attn",
    )(qi, small, q, kit, kt, vb)


SAMPLE_ROWS = 8
PAGES_PER_STEP = 8
HEAD_ROWS = N_HEADS * SAMPLE_ROWS


def _sample_scores_kernel(pt_ref, qi_ref, wcol_ref, kin_ref, *rest, page, past, n_new):
    pages = rest[:PAGES_PER_STEP]
    keys_ref = rest[PAGES_PER_STEP]
    st = pl.program_id(1)
    n_steps = pl.num_programs(1)
    qi = qi_ref[0]
    wcol = wcol_ref[0]
    lane = lax.broadcasted_iota(I32, (SAMPLE_ROWS, page), 1)
    row = lax.broadcasted_iota(I32, (SAMPLE_ROWS, page), 0)

    def scores(ki_bf16):
        d = jnp.maximum(_dot_nt(qi, ki_bf16), 0.0) * wcol
        sc = d[0:SAMPLE_ROWS]
        for h in range(1, N_IDX_HEADS):
            sc = sc + d[h * SAMPLE_ROWS:(h + 1) * SAMPLE_ROWS]
        return sc

    for p in range(PAGES_PER_STEP):
        off = pl.multiple_of((st * PAGES_PER_STEP + p) * page, page)
        keys_ref[:, pl.ds(off, page)] = _sort_key(scores(pages[p][0, 0].astype(BF16)), row < n_new)

    @pl.when(st == n_steps - 1)
    def _():
        adm = jnp.logical_and(row < n_new, lane <= row)
        keys_ref[:, pl.ds(past, page)] = _sort_key(scores(kin_ref[0]), adm)


def _sample_select_kernel(keys_ref, bias_ref, *, chunk, ksel, n_cols):
    rows = keys_ref.shape[0]
    n_chunks = n_cols // chunk
    thr, cut = _select_threshold(keys_ref, rows, chunk, n_chunks, ksel, n_cols)
    lane = lax.broadcasted_iota(I32, (rows, chunk), 1)

    def body(c, carry):
        off = pl.multiple_of(c * chunk, chunk)
        kc = keys_ref[:, pl.ds(off, chunk)]
        bias_ref[:, pl.ds(off, chunk)] = jnp.where(_selected(kc, off + lane, thr, cut), 0.0, NEG_BIG)
        return carry

    lax.fori_loop(0, n_chunks, body, 0)


def _sample_attn_kernel(pt_ref, q_ref, bias_ref, knew_ref, vnew_ref, *rest, page, past):
    kpages = rest[:PAGES_PER_STEP]
    vpages = rest[PAGES_PER_STEP:2 * PAGES_PER_STEP]
    o_ref = rest[2 * PAGES_PER_STEP]
    m_ref, l_ref, acc_ref, s_ref = rest[2 * PAGES_PER_STEP + 1:]
    st = pl.program_id(1)
    n_steps = pl.num_programs(1)
    q = q_ref[0]
    width = page * N_HEADS
    col = lax.broadcasted_iota(I32, (HEAD_ROWS, width), 1)
    rowi = lax.broadcasted_iota(I32, (HEAD_ROWS, width), 0)
    head_match = (col % N_HEADS) == (rowi // SAMPLE_ROWS)
    expand = (lax.broadcasted_iota(I32, (page, width), 1) // N_HEADS
              == lax.broadcasted_iota(I32, (page, width), 0)).astype(BF16)

    @pl.when(st == 0)
    def _():
        m_ref[...] = jnp.full_like(m_ref, NEG_BIG)
        l_ref[...] = jnp.zeros_like(l_ref)
        acc_ref[...] = jnp.zeros_like(acc_ref)

    def scores(kpage, bias_tok):
        s = _dot_nt(q, kpage)
        be = _dot(bias_tok.astype(BF16), expand)
        be = jnp.concatenate([be] * N_HEADS, axis=0)
        return jnp.where(head_match, s + be, NEG_BIG)

    def update(n_pages, vpage_of):
        m_old = m_ref[...]
        m_new = jnp.maximum(m_old, jnp.max(s_ref[:, 0:n_pages * width], axis=1, keepdims=True))
        alpha = jnp.exp(m_old - m_new)
        l_new = alpha * l_ref[...]
        acc = alpha * acc_ref[...]
        for p in range(n_pages):
            pp = jnp.exp(s_ref[:, p * width:(p + 1) * width] - m_new)
            l_new = l_new + jnp.sum(pp, axis=1, keepdims=True)
            acc = acc + _dot(pp.astype(BF16), vpage_of(p))
        m_ref[...] = m_new
        l_ref[...] = l_new
        acc_ref[...] = acc

    for p in range(PAGES_PER_STEP):
        off = pl.multiple_of((st * PAGES_PER_STEP + p) * page, page)
        s_ref[:, p * width:(p + 1) * width] = scores(kpages[p][0, 0].astype(BF16), bias_ref[0, :, pl.ds(off, page)])
    update(PAGES_PER_STEP, lambda p: vpages[p][0, 0].astype(BF16))

    @pl.when(st == n_steps - 1)
    def _():
        s_ref[:, 0:width] = scores(knew_ref[0], bias_ref[0, :, past:past + page])
        update(1, lambda p: vnew_ref[0])
        o_ref[0] = acc_ref[...] / l_ref[...]


def _sample_attention(layer, page_table, q, qi, small, kib_new, kb_new, vb_new, cache_k, cache_v, cache_idx_k):
    bd, n_new, _ = q.shape
    n_pages = page_table.shape[1]
    page = cache_idx_k.shape[2]
    past = n_pages * page
    ksel = min(TOPK_MAX, (past + n_new) // 4)
    n_steps = n_pages // PAGES_PER_STEP
    n_cols = past + page

    pad_tok = lambda a: jnp.pad(a, ((0, 0), (0, SAMPLE_ROWS - n_new), (0, 0)))
    head_major = lambda a: pad_tok(a).reshape(bd, SAMPLE_ROWS, N_HEADS, -1).transpose(0, 2, 1, 3)
    qi_rows = head_major(qi).reshape(bd, HEAD_ROWS, IDX_DIM)
    wcol = head_major(small[:, :, IDX_DIM:IDX_DIM + N_IDX_HEADS]).reshape(bd, HEAD_ROWS, 1)
    q_rows = head_major(q).reshape(bd, HEAD_ROWS, HEAD_DIM)
    width = page * N_HEADS
    kin_pad = jnp.pad(kib_new, ((0, 0), (0, page - n_new), (0, 0)))
    as_page = lambda a: jnp.pad(a.reshape(bd, n_new * N_HEADS, HEAD_DIM), ((0, 0), (0, width - n_new * N_HEADS), (0, 0)))
    knew_pad, vnew_pad = as_page(kb_new), as_page(vb_new)

    def page_spec(rows, p):
        return pl.BlockSpec((1, 1, rows, HEAD_DIM),
                            lambda b, st, pt, p=p: (layer, pt[b, st * PAGES_PER_STEP + p], 0, 0))

    per_b = lambda shape: pl.BlockSpec((1,) + shape, lambda b, st, pt: (b, 0, 0))

    keys = pl.pallas_call(
        functools.partial(_sample_scores_kernel, page=page, past=past, n_new=n_new),
        grid_spec=pltpu.PrefetchScalarGridSpec(
            num_scalar_prefetch=1,
            grid=(bd, n_steps),
            in_specs=[per_b((HEAD_ROWS, IDX_DIM)), per_b((HEAD_ROWS, 1)), per_b((page, IDX_DIM))]
                     + [page_spec(page, p) for p in range(PAGES_PER_STEP)],
            out_specs=pl.BlockSpec((SAMPLE_ROWS, n_cols), lambda b, st, pt: (b, 0)),
        ),
        out_shape=jax.ShapeDtypeStruct((bd * SAMPLE_ROWS, n_cols), I32),
        compiler_params=_cparams("parallel", "arbitrary"),
        name="sample_scores",
    )(page_table, qi_rows, wcol, kin_pad, *([cache_idx_k] * PAGES_PER_STEP))

    sel_rows = min(Q_ROWS, bd * SAMPLE_ROWS)
    bias = pl.pallas_call(
        functools.partial(_sample_select_kernel, chunk=page, ksel=ksel, n_cols=n_cols),
        grid=(bd * SAMPLE_ROWS // sel_rows,),
        in_specs=[pl.BlockSpec((sel_rows, n_cols), lambda i: (i, 0))],
        out_specs=pl.BlockSpec((sel_rows, n_cols), lambda i: (i, 0)),
        out_shape=jax.ShapeDtypeStruct((bd * SAMPLE_ROWS, n_cols), F32),
        compiler_params=_cparams("parallel"),
        name="sample_select",
    )(keys).reshape(bd, SAMPLE_ROWS, n_cols)

    out = pl.pallas_call(
        functools.partial(_sample_attn_kernel, page=page, past=past),
        grid_spec=pltpu.PrefetchScalarGridSpec(
            num_scalar_prefetch=1,
            grid=(bd, n_steps),
            in_specs=[per_b((HEAD_ROWS, HEAD_DIM)), per_b((SAMPLE_ROWS, n_cols)), per_b((width, HEAD_DIM)),
                      per_b((width, HEAD_DIM))] + [page_spec(width, p) for p in range(PAGES_PER_STEP)] * 2,
            out_specs=per_b((HEAD_ROWS, HEAD_DIM)),
            scratch_shapes=[pltpu.VMEM((HEAD_ROWS, 1), F32), pltpu.VMEM((HEAD_ROWS, 1), F32),
                            pltpu.VMEM((HEAD_ROWS, HEAD_DIM), F32),
                            pltpu.VMEM((HEAD_ROWS, PAGES_PER_STEP * width), F32)],
        ),
        out_shape=jax.ShapeDtypeStruct((bd, HEAD_ROWS, HEAD_DIM), F32),
        compiler_params=_cparams("parallel", "arbitrary"),
        name="sample_attn",
    )(page_table, q_rows, bias, knew_pad, vnew_pad, *([cache_k] * PAGES_PER_STEP), *([cache_v] * PAGES_PER_STEP))
    out = out.reshape(bd, N_HEADS, SAMPLE_ROWS, HEAD_DIM).transpose(0, 2, 1, 3).reshape(bd, SAMPLE_ROWS, ATTN_W)
    return out[:, :n_new]


CONV_HALO = 32


def _conv_prompt_kernel(u_ref, prev_ref, w_ref, b_ref, o_ref, ext_ref, *, tile):
    i = pl.program_id(1)
    halo = prev_ref[0, tile - CONV_HALO:tile, :]
    ext_ref[0:CONV_HALO, :] = jnp.where(i > 0, halo, 0.0)
    ext_ref[CONV_HALO:CONV_HALO + tile, :] = u_ref[0]
    acc = jnp.broadcast_to(b_ref[...], (tile, CONV_CH))
    base = CONV_HALO - (CONV_W - 1)
    for k in range(CONV_W):
        acc = acc + ext_ref[base + k:base + k + tile, :] * w_ref[k:k + 1, :]
    o_ref[0] = acc


def _conv_prompt(u, w, b, tile):
    bsz, s, c = u.shape
    return pl.pallas_call(
        functools.partial(_conv_prompt_kernel, tile=tile),
        grid=(bsz, s // tile),
        in_specs=[pl.BlockSpec((1, tile, c), lambda bi, i: (bi, i, 0)),
                  pl.BlockSpec((1, tile, c), lambda bi, i: (bi, jnp.maximum(i - 1, 0), 0)),
                  pl.BlockSpec(w.shape, lambda bi, i: (0, 0)),
                  pl.BlockSpec(b.shape, lambda bi, i: (0, 0))],
        out_specs=pl.BlockSpec((1, tile, c), lambda bi, i: (bi, i, 0)),
        out_shape=jax.ShapeDtypeStruct((bsz, s, c), F32),
        scratch_shapes=[pltpu.VMEM((CONV_HALO + tile, c), F32)],
        compiler_params=_cparams("parallel", "arbitrary"),
        name="conv_prompt",
    )(u, u, w, b)


def _conv_sample_kernel(state_ref, u_ref, w_ref, b_ref, o_ref):
    n_hist = state_ref.shape[0]
    n_new = u_ref.shape[0]
    for t in range(n_new):
        acc = jnp.broadcast_to(b_ref[...], o_ref.shape[1:])
        for j in range(t, n_hist):
            acc = acc + state_ref[j] * w_ref[j - t:j - t + 1, :]
        for i in range(t + 1):
            k = n_hist - t + i
            acc = acc + u_ref[i] * w_ref[k:k + 1, :]
        o_ref[t] = acc


def _conv_sample(state_t, u_t, w, b):
    return pl.pallas_call(
        _conv_sample_kernel,
        out_shape=jax.ShapeDtypeStruct(u_t.shape, F32),
        name="conv_sample",
    )(state_t, u_t, w, b)


def _merge_kernel(x_ref, attn_ref, conv_ref, ga_ref, gc_ref, wpa_ref, wpc_ref, wo_ref,
                  cg_ref, cb_ref, g1_ref, b1_ref, o_ref, *, alpha):
    c = _layer_norm(conv_ref[...], cg_ref[...], cb_ref[...])
    c = c * jax.nn.sigmoid(c)
    conv_out = _dot(c.astype(BF16), wpc_ref[...])
    attn_out = _dot(attn_ref[...].astype(BF16), wpa_ref[...])
    mix = ga_ref[...] * attn_out + gc_ref[...] * conv_out
    mix = _dot(mix.astype(BF16), wo_ref[...])
    o_ref[...] = _layer_norm(alpha * x_ref[...] + mix, g1_ref[...], b1_ref[...])


def _merge(x, attn, conv, ga, gc, lw, alpha, tm):
    t, d = x.shape
    rows = lambda a: pl.BlockSpec((tm, a.shape[1]), lambda i: (i, 0))
    full = lambda a: pl.BlockSpec(a.shape, lambda i: (0,) * a.ndim)
    weights = (lw['w_pa'], lw['w_pc'], lw['w_o'], lw['conv_ln_g'], lw['conv_ln_b'], lw['ln1_g'], lw['ln1_b'])
    acts = (x, attn, conv, ga, gc)
    return pl.pallas_call(
        functools.partial(_merge_kernel, alpha=alpha),
        grid=(t // tm,),
        in_specs=[rows(a) for a in acts] + [full(w) for w in weights],
        out_specs=pl.BlockSpec((tm, d), lambda i: (i, 0)),
        out_shape=jax.ShapeDtypeStruct((t, d), F32),
        compiler_params=_cparams("parallel"),
        name="merge",
    )(*acts, *weights)


EXPERTS_PER_STEP = 2


def _route(x, wr_hi_ref, wr_lo_ref, rbias_ref):
    tm = x.shape[0]
    x_hi = x.astype(BF16)
    x_lo = (x - x_hi.astype(F32)).astype(BF16)
    logits = _dot_nt(wr_hi_ref[...], x_hi) + (_dot_nt(wr_hi_ref[...], x_lo) + _dot_nt(wr_lo_ref[...], x_hi))
    scores = jax.nn.sigmoid(logits)
    sel = scores + rbias_ref[...]
    neg_inf = -jnp.inf
    sub8 = lax.broadcasted_iota(I32, (GROUP_SIZE, tm), 0).astype(F32)

    def take_first_max(cur, idx, n):
        m = jnp.max(cur, axis=0, keepdims=True)
        first = jnp.min(jnp.where(cur == m, idx, float(n)), axis=0, keepdims=True)
        return m, idx == first

    gs_rows = []
    for g in range(N_GROUPS):
        blk = sel[g * GROUP_SIZE:(g + 1) * GROUP_SIZE, :]
        m1, hit = take_first_max(blk, sub8, GROUP_SIZE)
        m2 = jnp.max(jnp.where(hit, neg_inf, blk), axis=0, keepdims=True)
        gs_rows.append(m1 + m2)
    gs = jnp.concatenate(gs_rows, axis=0)
    gsel = jnp.zeros((N_GROUPS, tm), jnp.bool_)
    for _ in range(TOPK_GROUPS):
        _, hit = take_first_max(gs, sub8, N_GROUPS)
        gsel = jnp.logical_or(gsel, hit)
        gs = jnp.where(hit, neg_inf, gs)
    gself = jnp.where(gsel, 1.0, 0.0)
    emask = jnp.concatenate(
        [jnp.broadcast_to(gself[g:g + 1, :], (GROUP_SIZE, tm)) for g in range(N_GROUPS)], axis=0) > 0.5
    cur = jnp.where(emask, sel, neg_inf)
    sub64 = lax.broadcasted_iota(I32, (N_EXPERTS, tm), 0).astype(F32)
    chosen = jnp.zeros((N_EXPERTS, tm), jnp.bool_)
    for _ in range(MOE_TOPK):
        _, hit = take_first_max(cur, sub64, N_EXPERTS)
        chosen = jnp.logical_or(chosen, hit)
        cur = jnp.where(hit, neg_inf, cur)
    w = jnp.where(chosen, scores, 0.0)
    w = w / jnp.sum(w, axis=0, keepdims=True) * ROUTED_SCALE
    return w.T


def _moe_kernel(x_ref, wrh_ref, wrl_ref, rb_ref, sg_ref, su_ref, sd_ref, eg_ref, eu_ref, ed_ref,
                g2_ref, b2_ref, o_ref, xb_ref, gate_ref, acc_ref, *, alpha):
    e = pl.program_id(1)

    def ffn(xb, wg, wu):
        hg = _dot(xb, wg)
        return hg * jax.nn.sigmoid(hg) * _dot(xb, wu)

    @pl.when(e == 0)
    def _():
        x = x_ref[...]
        xb = x.astype(BF16)
        xb_ref[...] = xb
        gate_ref[...] = _route(x, wrh_ref, wrl_ref, rb_ref)
        acc_ref[...] = _dot(ffn(xb, sg_ref[...], su_ref[...]).astype(BF16), sd_ref[...])

    xb = xb_ref[...]
    gate = gate_ref[...]
    lane = lax.broadcasted_iota(I32, gate.shape, 1)
    for i in range(EXPERTS_PER_STEP):
        gcol = jnp.sum(jnp.where(lane == e * EXPERTS_PER_STEP + i, gate, 0.0), axis=1, keepdims=True)
        h = ffn(xb, eg_ref[i], eu_ref[i]) * gcol
        acc_ref[...] += _dot(h.astype(BF16), ed_ref[i])

    @pl.when(e == pl.num_programs(1) - 1)
    def _():
        o_ref[...] = _layer_norm(alpha * x_ref[...] + acc_ref[...], g2_ref[...], b2_ref[...])


def _moe(x, lw, alpha, tm):
    t, d = x.shape
    full = lambda a: pl.BlockSpec(a.shape, lambda i, e: (0,) * a.ndim)
    exp_spec = lambda a: pl.BlockSpec((EXPERTS_PER_STEP,) + a.shape[1:], lambda i, e: (e, 0, 0))
    small_w = (lw['wr_hi'], lw['wr_lo'], lw['r_bias'], lw['w_sg'], lw['w_su'], lw['w_sd'])
    exp_w = (lw['w_eg'], lw['w_eu'], lw['w_ed'])
    ln_w = (lw['ln2_g'], lw['ln2_b'])
    return pl.pallas_call(
        functools.partial(_moe_kernel, alpha=alpha),
        grid=(t // tm, N_EXPERTS // EXPERTS_PER_STEP),
        in_specs=[pl.BlockSpec((tm, d), lambda i, e: (i, 0))] + [full(w) for w in small_w]
                 + [exp_spec(w) for w in exp_w] + [full(w) for w in ln_w],
        out_specs=pl.BlockSpec((tm, d), lambda i, e: (i, 0)),
        out_shape=jax.ShapeDtypeStruct((t, d), F32),
        scratch_shapes=[pltpu.VMEM((tm, d), BF16), pltpu.VMEM((tm, N_EXPERTS), F32), pltpu.VMEM((tm, d), F32)],
        compiler_params=_cparams("parallel", "arbitrary"),
        name="moe",
    )(x, *small_w, *exp_w, *ln_w)


def _rope_tables(pos):
    rd = HEAD_DIM // 4
    half = rd // 2
    inv = ROPE_THETA ** (-jnp.arange(half, dtype=F32) * 2.0 / rd)
    ang = pos.astype(F32)[:, None] * inv[None, :]
    cos, sin = jnp.cos(ang), jnp.sin(ang)
    t = pos.shape[0]
    ones = jnp.ones((t, HEAD_DIM - rd), F32)
    zeros = jnp.zeros((t, HEAD_DIM - rd), F32)
    zh = jnp.zeros((t, half), F32)
    c64 = jnp.concatenate([cos, cos, ones], axis=1)
    a64 = jnp.concatenate([-sin, zh, zeros], axis=1)
    b64 = jnp.concatenate([zh, sin, zeros], axis=1)
    rep = lambda a: jnp.tile(a, (1, LANES // HEAD_DIM))
    return rep(c64), rep(a64), rep(b64)


def _split_hi_lo(w):
    hi = w.astype(BF16)
    return hi, (w - hi.astype(F32)).astype(BF16)


def _row_tile(t, want):
    return want if t % want == 0 else t


def kernel(x_prompt, x_sample, cache_k, cache_v, cache_idx_k, state_conv, page_table, w_in, b_in, idx_k_ln_g, idx_k_ln_b, conv_w, conv_b, conv_ln_g, conv_ln_b, w_pa, w_pc, w_o, ln1_g, ln1_b, w_router, router_bias, w_exp_gate, w_exp_up, w_exp_down, w_sh_gate, w_sh_up, w_sh_down, ln2_g, ln2_b):
    bsz, seq, d = x_prompt.shape
    bd, n_new, _ = x_sample.shape
    depth = w_in.shape[0]
    n_pool, page = cache_k.shape[1], cache_k.shape[2]
    past = page_table.shape[1] * page
    alpha = (2 * depth) ** 0.25
    assert seq % KEY_CHUNK == 0 or seq < KEY_CHUNK
    assert page == LANES and page_table.shape[1] % PAGES_PER_STEP == 0 and n_new <= SAMPLE_ROWS

    c_a = 4 * ATTN_W
    c_s = c_a + IDX_DIM + N_IDX_HEADS
    c_g = c_s + 2 * CONV_CH
    pad_s = LANES - (IDX_DIM + N_IDX_HEADS)
    row2 = lambda a: a.reshape(1, -1)

    cache_k2 = cache_k.reshape(depth, n_pool, page * N_HEADS, HEAD_DIM)
    cache_v2 = cache_v.reshape(depth, n_pool, page * N_HEADS, HEAD_DIM)
    tables_p = _rope_tables(jnp.arange(seq))
    tables_p = tuple(jnp.tile(a, (bsz, 1)) for a in tables_p)
    tables_s = _rope_tables(past + jnp.arange(n_new))
    tables_s = tuple(jnp.tile(a, (bd, 1)) for a in tables_s)
    state_t = state_conv.transpose(0, 2, 1, 3)

    xp = x_prompt.reshape(bsz * seq, d)
    xs = x_sample.reshape(bd * n_new, d)
    outs = {name: [] for name in ('kp', 'vp', 'kip', 'cp', 'ks', 'vs', 'kis', 'cs')}
    tm_p = _row_tile(bsz * seq, 256)
    tm_s = bd * n_new
    tm_moe = _row_tile(bsz * seq, 1024)

    for l in range(depth):
        wr_hi, wr_lo = _split_hi_lo(w_router[l].T)
        lw = {
            'w_a': w_in[l, :, :c_a].astype(BF16),
            'w_s': jnp.pad(w_in[l, :, c_a:c_s], ((0, 0), (0, pad_s))).astype(BF16),
            'w_g': w_in[l, :, c_s:c_g].astype(BF16),
            'w_t': w_in[l, :, c_g:].astype(BF16),
            'b_a': row2(b_in[l, :c_a]),
            'b_s': row2(jnp.pad(b_in[l, c_a:c_s], (0, pad_s))),
            'b_g': row2(b_in[l, c_s:c_g]),
            'b_t': row2(b_in[l, c_g:]),
            'idx_ln_g': row2(jnp.pad(idx_k_ln_g[l], (0, LANES - IDX_DIM))),
            'idx_ln_b': row2(jnp.pad(idx_k_ln_b[l], (0, LANES - IDX_DIM))),
            'w_pa': w_pa[l].astype(BF16), 'w_pc': w_pc[l].astype(BF16), 'w_o': w_o[l].astype(BF16),
            'conv_ln_g': row2(conv_ln_g[l]), 'conv_ln_b': row2(conv_ln_b[l]),
            'ln1_g': row2(ln1_g[l]), 'ln1_b': row2(ln1_b[l]),
            'wr_hi': wr_hi, 'wr_lo': wr_lo, 'r_bias': router_bias[l].reshape(-1, 1),
            'w_sg': w_sh_gate[l].astype(BF16), 'w_su': w_sh_up[l].astype(BF16), 'w_sd': w_sh_down[l].astype(BF16),
            'w_eg': w_exp_gate[l].astype(BF16), 'w_eu': w_exp_up[l].astype(BF16), 'w_ed': w_exp_down[l].astype(BF16),
            'ln2_g': row2(ln2_g[l]), 'ln2_b': row2(ln2_b[l]),
        }
        cw = jnp.pad(conv_w[l], ((0, 1), (0, 0)))
        cb = row2(conv_b[l])

        q, kf, kb, vf, vb, qi, small, kib, u, ga, gc, kt, kit = _inproj(xp, lw, tables_p, tm_p)
        r3 = lambda a: a.reshape(bsz, seq, a.shape[-1])
        attn = _prompt_attn(r3(qi), r3(small), r3(q), kit, kt, r3(vb))
        u3 = r3(u)
        conv = _conv_prompt(u3, cw, cb, min(512, seq))
        outs['kp'].append(kf.reshape(bsz, seq, N_HEADS, HEAD_DIM))
        outs['vp'].append(vf.reshape(bsz, seq, N_HEADS, HEAD_DIM))
        outs['kip'].append(r3(small)[:, :, :IDX_DIM])
        outs['cp'].append(u3[:, seq - (CONV_W - 1):])
        x1 = _merge(xp, attn.reshape(bsz * seq, ATTN_W), conv.reshape(bsz * seq, CONV_CH), ga, gc, lw, alpha, tm_p)
        xp = _moe(x1, lw, alpha, tm_moe)

        q, kf, kb, vf, vb, qi, small, kib, u, ga, gc, _, _ = _inproj(xs, lw, tables_s, tm_s)
        s3 = lambda a: a.reshape(bd, n_new, a.shape[-1])
        attn = _sample_attention(l, page_table, s3(q), s3(qi), s3(small), s3(kib), s3(kb), s3(vb),
                                 cache_k2, cache_v2, cache_idx_k)
        u3 = s3(u)
        conv = _conv_sample(state_t[l], u3.transpose(1, 0, 2), cw, cb).transpose(1, 0, 2)
        outs['ks'].append(kf.reshape(bd, n_new, N_HEADS, HEAD_DIM))
        outs['vs'].append(vf.reshape(bd, n_new, N_HEADS, HEAD_DIM))
        outs['kis'].append(s3(small)[:, :, :IDX_DIM])
        outs['cs'].append(jnp.concatenate([state_conv[l][:, n_new:], u3], axis=1))
        x1 = _merge(xs, attn.reshape(bd * n_new, ATTN_W), conv.reshape(bd * n_new, CONV_CH), ga, gc, lw, alpha, tm_s)
        xs = _moe(x1, lw, alpha, tm_s)

    st = lambda name: jnp.stack(outs[name])
    return (xp.reshape(bsz, seq, d), xs.reshape(bd, n_new, d), st('kp'), st('vp'), st('kip'), st('cp'),
            st('ks'), st('vs'), st('kis'), st('cs'))
```

```python
import functools
import math

import jax
import jax.numpy as jnp
from jax import lax
from jax.experimental import pallas as pl
from jax.experimental.pallas import tpu as pltpu

F32 = jnp.float32
BF16 = jnp.bfloat16
I32 = jnp.int32

N_HEADS = 8
HEAD_DIM = 64
ATTN_W = N_HEADS * HEAD_DIM
N_IDX_HEADS = 8
IDX_DIM = 64
IDX_W_SCALE = (N_IDX_HEADS * IDX_DIM) ** -0.5
TOPK_MAX = 256
ROPE_THETA = 500000.0
CONV_CH = 512
CONV_W = 31
N_EXPERTS = 64
EXPERT_FF = 256
MOE_TOPK = 8
N_GROUPS = 8
GROUP_SIZE = N_EXPERTS // N_GROUPS
TOPK_GROUPS = 4
ROUTED_SCALE = 2.5
LN_EPS = 1e-5

LANES = 128
Q_ROWS = 128
KEY_CHUNK = 512
ATTN_CHUNK = 256
INT_MIN = -(2 ** 31)
NEG_BIG = -1e30
VMEM_LIMIT = 56 * 1024 * 1024


def _cparams(*sem):
    return pltpu.CompilerParams(dimension_semantics=sem, vmem_limit_bytes=VMEM_LIMIT)


def _layer_norm(x, g, b):
    mu = jnp.mean(x, axis=-1, keepdims=True)
    xc = x - mu
    var = jnp.mean(xc * xc, axis=-1, keepdims=True)
    return xc * lax.rsqrt(var + LN_EPS) * g + b


def _dot(a, b):
    return jnp.dot(a, b, preferred_element_type=F32)


def _dot_nt(a, b):
    return lax.dot_general(a, b, (((1,), (1,)), ((), ())), preferred_element_type=F32)


def _inproj_kernel(x_ref, wa_ref, ws_ref, wg_ref, wt_ref, ba_ref, bs_ref, bg_ref, bt_ref,
                   lng_ref, lnb_ref, cos_ref, s1_ref, s2_ref,
                   q_ref, kf_ref, vf_ref, vb_ref, qi_ref, small_ref,
                   u_ref, ga_ref, gc_ref, kt_ref, kit_ref):
    xb = x_ref[...].astype(BF16)
    cos = cos_ref[...]
    s1 = s1_ref[...]
    s2 = s2_ref[...]

    def rope128(v, c, a, b):
        return v * c + pltpu.roll(v, LANES - 8, 1) * a + pltpu.roll(v, 8, 1) * b

    def rope_seg(seg):
        return jnp.concatenate(
            [rope128(seg[:, c * LANES:(c + 1) * LANES], cos, s1, s2) for c in range(ATTN_W // LANES)], axis=1)

    ha = _dot(xb, wa_ref[...]) + ba_ref[...]
    q = rope_seg(ha[:, 0:ATTN_W])
    k = rope_seg(ha[:, ATTN_W:2 * ATTN_W])
    v = ha[:, 2 * ATTN_W:3 * ATTN_W]
    qi = rope_seg(ha[:, 3 * ATTN_W:4 * ATTN_W])
    q_ref[...] = (q * (HEAD_DIM ** -0.5)).astype(BF16)
    kf_ref[...] = k
    kt_ref[...] = k.T.astype(BF16)
    vf_ref[...] = v
    vb_ref[...] = v.astype(BF16)
    qi_ref[...] = qi.astype(BF16)

    hs = _dot(xb, ws_ref[...]) + bs_ref[...]
    lane = lax.broadcasted_iota(I32, hs.shape, 1)
    is_key = lane < IDX_DIM
    mu = jnp.sum(jnp.where(is_key, hs, 0.0), axis=1, keepdims=True) * (1.0 / IDX_DIM)
    hc = jnp.where(is_key, hs - mu, 0.0)
    var = jnp.sum(hc * hc, axis=1, keepdims=True) * (1.0 / IDX_DIM)
    kin = hc * lax.rsqrt(var + LN_EPS) * lng_ref[...] + lnb_ref[...]
    kir = rope128(kin, jnp.where(is_key, cos, 1.0), jnp.where(is_key, s1, 0.0), jnp.where(is_key, s2, 0.0))
    small = jnp.where(is_key, kir, jnp.where(lane < IDX_DIM + N_IDX_HEADS, hs * IDX_W_SCALE, 0.0))
    small_ref[...] = small
    kit = small.T[0:IDX_DIM, :].astype(BF16)
    kit_ref[...] = jnp.concatenate([kit, kit], axis=0)

    hg = _dot(xb, wg_ref[...]) + bg_ref[...]
    u_ref[...] = hg[:, 0:CONV_CH] * jax.nn.sigmoid(hg[:, CONV_CH:2 * CONV_CH])

    d = ga_ref.shape[1]
    ht = jax.nn.sigmoid(_dot(xb, wt_ref[...]) + bt_ref[...])
    ga_ref[...] = ht[:, 0:d]
    gc_ref[...] = ht[:, d:2 * d]


def _inproj(x, lw, tables, tm):
    t, d = x.shape
    cos, s1, s2 = tables
    full = lambda a: pl.BlockSpec(a.shape, lambda i: (0,) * a.ndim)
    rows = lambda w: pl.BlockSpec((tm, w), lambda i: (i, 0))
    weights = (lw['w_a'], lw['w_s'], lw['w_g'], lw['w_t'], lw['b_a'], lw['b_s'], lw['b_g'], lw['b_t'],
               lw['idx_ln_g'], lw['idx_ln_b'])
    out_shapes = (
        jax.ShapeDtypeStruct((t, ATTN_W), BF16),
        jax.ShapeDtypeStruct((t, ATTN_W), F32),
        jax.ShapeDtypeStruct((t, ATTN_W), F32),
        jax.ShapeDtypeStruct((t, ATTN_W), BF16),
        jax.ShapeDtypeStruct((t, ATTN_W), BF16),
        jax.ShapeDtypeStruct((t, LANES), F32),
        jax.ShapeDtypeStruct((t, CONV_CH), F32),
        jax.ShapeDtypeStruct((t, d), F32),
        jax.ShapeDtypeStruct((t, d), F32),
    )
    cols_shapes = (
        jax.ShapeDtypeStruct((ATTN_W, t), BF16),
        jax.ShapeDtypeStruct((LANES, t), BF16),
    )
    cols = lambda h: pl.BlockSpec((h, tm), lambda i: (0, i))
    return pl.pallas_call(
        _inproj_kernel,
        grid=(t // tm,),
        in_specs=[rows(d)] + [full(w) for w in weights] + [rows(LANES)] * 3,
        out_specs=tuple(rows(s.shape[1]) for s in out_shapes) + tuple(cols(s.shape[0]) for s in cols_shapes),
        out_shape=out_shapes + cols_shapes,
        compiler_params=_cparams("parallel"),
        name="inproj",
    )(x, *weights, cos, s1, s2)


def _sort_key(score, admissible):
    bits = pltpu.bitcast(score, I32)
    bits = jnp.where(score == 0.0, 0, bits)
    key = jnp.where(bits >= 0, bits, bits ^ 0x7FFFFFFF)
    return jnp.where(admissible, key, INT_MIN)


def _fold_lanes(x):
    w = x.shape[1]
    acc = x[:, 0:LANES]
    for c in range(1, w // LANES):
        acc = acc + x[:, c * LANES:(c + 1) * LANES]
    return acc


def _select_threshold(keys_ref, rows, chunk, n_chunks, ksel, n_cols):
    lane = lax.broadcasted_iota(I32, (rows, chunk), 1)

    def count(pred):
        def body(c, cnt):
            off = pl.multiple_of(c * chunk, chunk)
            kc = keys_ref[:, pl.ds(off, chunk)]
            return cnt + _fold_lanes(jnp.where(pred(kc, off), 1, 0))
        cnt = lax.fori_loop(0, n_chunks, body, jnp.zeros((rows, LANES), I32))
        return jnp.sum(cnt.astype(F32), axis=1, keepdims=True).astype(I32)

    cnt0 = count(lambda kc, off: kc >= 0)
    thr0 = jnp.where(cnt0 >= ksel, 0, INT_MIN).astype(I32)

    def bit_body(i, thr):
        cand = thr | lax.shift_left(jnp.int32(1), 30 - i)
        cnt = count(lambda kc, off: kc >= cand)
        return jnp.where(cnt >= ksel, cand, thr)

    thr = lax.fori_loop(0, 31, bit_body, thr0)

    cnt_gt = count(lambda kc, off: kc > thr)
    cnt_ge = count(lambda kc, off: kc >= thr)
    need = ksel - cnt_gt
    live = thr != INT_MIN
    over = jnp.logical_and(live, cnt_ge > ksel)
    n_bits = max(1, (n_cols - 1).bit_length())

    def tie_search():
        def body(i, m):
            cand = m | lax.shift_left(jnp.int32(1), n_bits - 1 - i)
            g = count(lambda kc, off: jnp.logical_and(kc == thr, (off + lane) < cand))
            return jnp.where(g < need, cand, m)
        return lax.fori_loop(0, n_bits, body, jnp.zeros((rows, 1), I32))

    any_over = jnp.max(jnp.where(over, 1.0, 0.0)) > 0.5
    m = lax.cond(any_over, tie_search, lambda: jnp.zeros((rows, 1), I32))
    cut = jnp.where(live, jnp.where(over, m, n_cols), -1).astype(I32)
    return thr, cut


def _selected(kc, cols, thr, cut):
    return jnp.logical_or(kc > thr, jnp.logical_and(kc == thr, cols <= cut))


def _head_masked(x):
    lane = lax.broadcasted_iota(I32, (x.shape[0], LANES), 1)
    out = []
    for h in range(N_HEADS):
        pair = x[:, (h // 2) * LANES:(h // 2 + 1) * LANES]
        keep = (lane < HEAD_DIM) if h % 2 == 0 else (lane >= HEAD_DIM)
        out.append(jnp.where(keep, pair, jnp.zeros_like(pair)))
    return out


def _prompt_attn_kernel(qi_ref, small_ref, q_ref, kit_ref, kt_ref, v_ref, o_ref, keys_ref, m_ref, l_ref, acc_ref,
                        *, seq, chunk, achunk, ksel):
    j = pl.program_id(1)
    n_chunks = (j * Q_ROWS + Q_ROWS + chunk - 1) // chunk
    wi = small_ref[0]
    row = j * Q_ROWS + lax.broadcasted_iota(I32, (Q_ROWS, chunk), 0)
    lane = lax.broadcasted_iota(I32, (Q_ROWS, chunk), 1)
    wcols = [wi[:, IDX_DIM + h:IDX_DIM + h + 1] for h in range(N_IDX_HEADS)]
    qi_h = _head_masked(qi_ref[0])

    def idx_body(c, carry):
        off = pl.multiple_of(c * chunk, chunk)
        kic = kit_ref[:, pl.ds(off, chunk)]
        score = jnp.zeros((Q_ROWS, chunk), F32)
        for h in range(N_IDX_HEADS):
            score = score + jnp.maximum(_dot(qi_h[h], kic), 0.0) * wcols[h]
        keys_ref[:, pl.ds(off, chunk)] = _sort_key(score, (off + lane) <= row)
        return carry

    lax.fori_loop(0, n_chunks, idx_body, 0)
    thr, cut = _select_threshold(keys_ref, Q_ROWS, chunk, n_chunks, ksel, seq)

    q_h = _head_masked(q_ref[0])
    n_ach = (j * Q_ROWS + Q_ROWS + achunk - 1) // achunk
    lane_a = lax.broadcasted_iota(I32, (Q_ROWS, achunk), 1)
    even = lax.broadcasted_iota(I32, (Q_ROWS, LANES), 1) < HEAD_DIM
    ones = jnp.ones((achunk, LANES), BF16)
    m_ref[...] = jnp.full(m_ref.shape, NEG_BIG, F32)
    l_ref[...] = jnp.zeros_like(l_ref)
    acc_ref[...] = jnp.zeros_like(acc_ref)

    def attn_body(c, carry):
        off = pl.multiple_of(c * achunk, achunk)
        kc = keys_ref[:, pl.ds(off, achunk)]
        bias = jnp.where(_selected(kc, off + lane_a, thr, cut), 0.0, NEG_BIG)
        for pr in range(N_HEADS // 2):
            sl = slice(pr * LANES, (pr + 1) * LANES)
            kk = kt_ref[sl, pl.ds(off, achunk)]
            vv = jnp.concatenate([v_ref[0, pl.ds(off, achunk), sl], ones], axis=1)
            pvs, alphas = [], []
            for h in (2 * pr, 2 * pr + 1):
                hl = slice(h * LANES, (h + 1) * LANES)
                s = _dot(q_h[h], kk) + bias
                m_old = m_ref[:, hl]
                m_new = jnp.maximum(m_old, jnp.max(s, axis=1, keepdims=True))
                m_ref[:, hl] = m_new
                alpha = jnp.exp(m_old - m_new)
                p = jnp.exp(s - jnp.concatenate([m_new] * (achunk // LANES), axis=1))
                pv = _dot(p.astype(BF16), vv)
                l_ref[:, hl] = alpha * l_ref[:, hl] + pv[:, LANES:]
                pvs.append(pv[:, :LANES])
                alphas.append(alpha)
            acc_ref[:, sl] = (jnp.where(even, alphas[0], alphas[1]) * acc_ref[:, sl]
                              + jnp.where(even, pvs[0], pvs[1]))
        return carry

    lax.fori_loop(0, n_ach, attn_body, 0)
    for pr in range(N_HEADS // 2):
        sl = slice(pr * LANES, (pr + 1) * LANES)
        l_pair = jnp.where(even, l_ref[:, 2 * pr * LANES:(2 * pr + 1) * LANES],
                           l_ref[:, (2 * pr + 1) * LANES:(2 * pr + 2) * LANES])
        o_ref[0, :, sl] = (acc_ref[:, sl] / l_pair).astype(o_ref.dtype)


def _prompt_attn(qi, small, q, kit, kt, vb):
    b, s, _ = q.shape
    chunk = min(KEY_CHUNK, s)
    achunk = min(ATTN_CHUNK, s)
    ksel = min(TOPK_MAX, s // 4)
    qblk = lambda w: pl.BlockSpec((1, Q_ROWS, w), lambda bi, j: (bi, j, 0))
    return pl.pallas_call(
        functools.partial(_prompt_attn_kernel, seq=s, chunk=chunk, achunk=achunk, ksel=ksel),
        grid=(b, s // Q_ROWS),
        in_specs=[qblk(ATTN_W), qblk(LANES), qblk(ATTN_W),
                  pl.BlockSpec((LANES, s), lambda bi, j: (0, bi)),
                  pl.BlockSpec((ATTN_W, s), lambda bi, j: (0, bi)),
                  pl.BlockSpec((1, s, ATTN_W), lambda bi, j: (bi, 0, 0))],
        out_specs=qblk(ATTN_W),
        out_shape=jax.ShapeDtypeStruct((b, s, ATTN_W), BF16),
        scratch_shapes=[pltpu.VMEM((Q_ROWS, s), I32), pltpu.VMEM((Q_ROWS, N_HEADS * LANES), F32),
                        pltpu.VMEM((Q_ROWS, N_HEADS * LANES), F32), pltpu.VMEM((Q_ROWS, ATTN_W), F32)],
        compiler_params=_cparams("parallel", "arbitrary"),
        name="prompt_attn",
    )(qi, small, q, kit, kt, vb)


SAMPLE_ROWS = 8
PAGES_PER_STEP = 8
HEAD_ROWS = N_HEADS * SAMPLE_ROWS


def _sample_scores_kernel(pt_ref, qi_ref, wcol_ref, kin_ref, *rest, page, past, n_new):
    pages = rest[:PAGES_PER_STEP]
    keys_ref = rest[PAGES_PER_STEP]
    st = pl.program_id(1)
    n_steps = pl.num_programs(1)
    qi = qi_ref[0]
    wcol = wcol_ref[0]
    lane = lax.broadcasted_iota(I32, (SAMPLE_ROWS, page), 1)
    row = lax.broadcasted_iota(I32, (SAMPLE_ROWS, page), 0)

    def scores(kit_bf16):
        d = jnp.maximum(_dot(qi, kit_bf16), 0.0) * wcol
        sc = d[0:SAMPLE_ROWS]
        for h in range(1, N_IDX_HEADS):
            sc = sc + d[h * SAMPLE_ROWS:(h + 1) * SAMPLE_ROWS]
        return sc

    for p in range(PAGES_PER_STEP):
        off = pl.multiple_of((st * PAGES_PER_STEP + p) * page, page)
        keys_ref[:, pl.ds(off, page)] = _sort_key(scores(pages[p][0, 0].astype(BF16)), row < n_new)

    @pl.when(st == n_steps - 1)
    def _():
        adm = jnp.logical_and(row < n_new, lane <= row)
        keys_ref[:, pl.ds(past, page)] = _sort_key(scores(kin_ref[0]), adm)


def _sample_select_kernel(keys_ref, bias_ref, *, chunk, ksel, n_cols):
    rows = keys_ref.shape[0]
    n_chunks = n_cols // chunk
    thr, cut = _select_threshold(keys_ref, rows, chunk, n_chunks, ksel, n_cols)
    lane = lax.broadcasted_iota(I32, (rows, chunk), 1)

    def body(c, carry):
        off = pl.multiple_of(c * chunk, chunk)
        kc = keys_ref[:, pl.ds(off, chunk)]
        bias_ref[:, pl.ds(off, chunk)] = jnp.where(_selected(kc, off + lane, thr, cut), 0.0, NEG_BIG)
        return carry

    lax.fori_loop(0, n_chunks, body, 0)


def _sample_attn_kernel(pt_ref, q_ref, bias_ref, knew_ref, vnew_ref, *rest, page, past):
    kpages = rest[:PAGES_PER_STEP]
    vpages = rest[PAGES_PER_STEP:2 * PAGES_PER_STEP]
    o_ref = rest[2 * PAGES_PER_STEP]
    m_ref, l_ref, acc_ref = rest[2 * PAGES_PER_STEP + 1:]
    st = pl.program_id(1)
    n_steps = pl.num_programs(1)
    q = q_ref[0]

    @pl.when(st == 0)
    def _():
        m_ref[...] = jnp.full_like(m_ref, NEG_BIG)
        l_ref[...] = jnp.zeros_like(l_ref)
        acc_ref[...] = jnp.zeros_like(acc_ref)

    def scores(kpage_t, bias_tok):
        return _dot(q, kpage_t) + jnp.concatenate([bias_tok] * N_HEADS, axis=0)

    def update(s_list, v_list):
        m_old = m_ref[...]
        m_new = m_old
        for s in s_list:
            m_new = jnp.maximum(m_new, jnp.max(s, axis=1, keepdims=True))
        alpha = jnp.exp(m_old - m_new)
        l_new = alpha * l_ref[...]
        acc = alpha * acc_ref[...]
        for s, vpage_t in zip(s_list, v_list):
            pp = jnp.exp(s - m_new)
            l_new = l_new + jnp.sum(pp, axis=1, keepdims=True)
            acc = acc + _dot_nt(pp.astype(BF16), vpage_t)
        m_ref[...] = m_new
        l_ref[...] = l_new
        acc_ref[...] = acc

    as_rows = lambda ref: ref[0, 0].reshape(ATTN_W, page).astype(BF16)
    s_list = []
    for p in range(PAGES_PER_STEP):
        off = pl.multiple_of((st * PAGES_PER_STEP + p) * page, page)
        s_list.append(scores(as_rows(kpages[p]), bias_ref[0, :, pl.ds(off, page)]))
    update(s_list, [as_rows(vpages[p]) for p in range(PAGES_PER_STEP)])

    @pl.when(st == n_steps - 1)
    def _():
        update([scores(knew_ref[0], bias_ref[0, :, past:past + page])], [vnew_ref[0]])
        out = acc_ref[...] / l_ref[...]
        for h in range(N_HEADS):
            hs = slice(h * HEAD_DIM, (h + 1) * HEAD_DIM)
            o_ref[0, :, hs] = out[h * SAMPLE_ROWS:(h + 1) * SAMPLE_ROWS, hs]


def _sample_attention(layer, page_table, q, qi, small, kit_new, kt_new, vb_new, cache_kt, cache_vt, cache_idx_kt):
    bd, n_new, _ = q.shape
    n_pages = page_table.shape[1]
    page = cache_idx_kt.shape[3]
    past = n_pages * page
    ksel = min(TOPK_MAX, (past + n_new) // 4)
    n_steps = n_pages // PAGES_PER_STEP
    n_cols = past + page

    pad_tok = lambda a: jnp.pad(a, ((0, 0), (0, SAMPLE_ROWS - n_new), (0, 0)))
    head_major = lambda a: pad_tok(a).reshape(bd, SAMPLE_ROWS, N_HEADS, -1).transpose(0, 2, 1, 3)
    qi_rows = head_major(qi).reshape(bd, HEAD_ROWS, IDX_DIM)
    wcol = head_major(small[:, :, IDX_DIM:IDX_DIM + N_IDX_HEADS]).reshape(bd, HEAD_ROWS, 1)
    qh = head_major(q)
    eye = jnp.eye(N_HEADS, dtype=q.dtype)
    q_rows = (qh[:, :, :, None, :] * eye[None, :, None, :, None]).reshape(bd, HEAD_ROWS, ATTN_W)
    new_page = lambda a_t: jnp.pad(a_t.reshape(-1, bd, n_new).transpose(1, 0, 2), ((0, 0), (0, 0), (0, page - n_new)))
    kin_pad, knew_pad = new_page(kit_new), new_page(kt_new)
    vnew_pad = jnp.pad(vb_new.transpose(0, 2, 1), ((0, 0), (0, 0), (0, page - n_new)))

    def kv_page_spec(p):
        return pl.BlockSpec((1, 1, N_HEADS, HEAD_DIM, page),
                            lambda b, st, pt, p=p: (layer, pt[b, st * PAGES_PER_STEP + p], 0, 0, 0))

    def idx_page_spec(p):
        return pl.BlockSpec((1, 1, IDX_DIM, page),
                            lambda b, st, pt, p=p: (layer, pt[b, st * PAGES_PER_STEP + p], 0, 0))

    per_b = lambda shape: pl.BlockSpec((1,) + shape, lambda b, st, pt: (b, 0, 0))

    keys = pl.pallas_call(
        functools.partial(_sample_scores_kernel, page=page, past=past, n_new=n_new),
        grid_spec=pltpu.PrefetchScalarGridSpec(
            num_scalar_prefetch=1,
            grid=(bd, n_steps),
            in_specs=[per_b((HEAD_ROWS, IDX_DIM)), per_b((HEAD_ROWS, 1)), per_b((IDX_DIM, page))]
                     + [idx_page_spec(p) for p in range(PAGES_PER_STEP)],
            out_specs=pl.BlockSpec((SAMPLE_ROWS, n_cols), lambda b, st, pt: (b, 0)),
        ),
        out_shape=jax.ShapeDtypeStruct((bd * SAMPLE_ROWS, n_cols), I32),
        compiler_params=_cparams("parallel", "arbitrary"),
        name="sample_scores",
    )(page_table, qi_rows, wcol, kin_pad, *([cache_idx_kt] * PAGES_PER_STEP))

    sel_rows = min(Q_ROWS, bd * SAMPLE_ROWS)
    bias = pl.pallas_call(
        functools.partial(_sample_select_kernel, chunk=page, ksel=ksel, n_cols=n_cols),
        grid=(bd * SAMPLE_ROWS // sel_rows,),
        in_specs=[pl.BlockSpec((sel_rows, n_cols), lambda i: (i, 0))],
        out_specs=pl.BlockSpec((sel_rows, n_cols), lambda i: (i, 0)),
        out_shape=jax.ShapeDtypeStruct((bd * SAMPLE_ROWS, n_cols), F32),
        compiler_params=_cparams("parallel"),
        name="sample_select",
    )(keys).reshape(bd, SAMPLE_ROWS, n_cols)

    out = pl.pallas_call(
        functools.partial(_sample_attn_kernel, page=page, past=past),
        grid_spec=pltpu.PrefetchScalarGridSpec(
            num_scalar_prefetch=1,
            grid=(bd, n_steps),
            in_specs=[per_b((HEAD_ROWS, ATTN_W)), per_b((SAMPLE_ROWS, n_cols)), per_b((ATTN_W, page)),
                      per_b((ATTN_W, page))] + [kv_page_spec(p) for p in range(PAGES_PER_STEP)] * 2,
            out_specs=per_b((SAMPLE_ROWS, ATTN_W)),
            scratch_shapes=[pltpu.VMEM((HEAD_ROWS, 1), F32), pltpu.VMEM((HEAD_ROWS, 1), F32),
                            pltpu.VMEM((HEAD_ROWS, ATTN_W), F32)],
        ),
        out_shape=jax.ShapeDtypeStruct((bd, SAMPLE_ROWS, ATTN_W), F32),
        compiler_params=_cparams("parallel", "arbitrary"),
        name="sample_attn",
    )(page_table, q_rows, bias, knew_pad, vnew_pad, *([cache_kt] * PAGES_PER_STEP), *([cache_vt] * PAGES_PER_STEP))
    return out[:, :n_new]


CONV_HALO = 32


def _conv_prompt_kernel(u_ref, prev_ref, w_ref, b_ref, o_ref, ext_ref, *, tile):
    i = pl.program_id(1)
    halo = prev_ref[0, tile - CONV_HALO:tile, :]
    ext_ref[0:CONV_HALO, :] = jnp.where(i > 0, halo, 0.0)
    ext_ref[CONV_HALO:CONV_HALO + tile, :] = u_ref[0]
    acc = jnp.broadcast_to(b_ref[...], (tile, CONV_CH))
    base = CONV_HALO - (CONV_W - 1)
    for k in range(CONV_W):
        acc = acc + ext_ref[base + k:base + k + tile, :] * w_ref[k:k + 1, :]
    o_ref[0] = acc


def _conv_prompt(u, w, b, tile):
    bsz, s, c = u.shape
    return pl.pallas_call(
        functools.partial(_conv_prompt_kernel, tile=tile),
        grid=(bsz, s // tile),
        in_specs=[pl.BlockSpec((1, tile, c), lambda bi, i: (bi, i, 0)),
                  pl.BlockSpec((1, tile, c), lambda bi, i: (bi, jnp.maximum(i - 1, 0), 0)),
                  pl.BlockSpec(w.shape, lambda bi, i: (0, 0)),
                  pl.BlockSpec(b.shape, lambda bi, i: (0, 0))],
        out_specs=pl.BlockSpec((1, tile, c), lambda bi, i: (bi, i, 0)),
        out_shape=jax.ShapeDtypeStruct((bsz, s, c), F32),
        scratch_shapes=[pltpu.VMEM((CONV_HALO + tile, c), F32)],
        compiler_params=_cparams("parallel", "arbitrary"),
        name="conv_prompt",
    )(u, u, w, b)


def _conv_sample_kernel(state_ref, u_ref, w_ref, b_ref, o_ref):
    n_hist = state_ref.shape[0]
    n_new = u_ref.shape[0]
    for t in range(n_new):
        acc = jnp.broadcast_to(b_ref[...], o_ref.shape[1:])
        for j in range(t, n_hist):
            acc = acc + state_ref[j] * w_ref[j - t:j - t + 1, :]
        for i in range(t + 1):
            k = n_hist - t + i
            acc = acc + u_ref[i] * w_ref[k:k + 1, :]
        o_ref[t] = acc


def _conv_sample(state_t, u_t, w, b):
    return pl.pallas_call(
        _conv_sample_kernel,
        out_shape=jax.ShapeDtypeStruct(u_t.shape, F32),
        name="conv_sample",
    )(state_t, u_t, w, b)


def _merge_kernel(x_ref, attn_ref, conv_ref, ga_ref, gc_ref, wpa_ref, wpc_ref, wo_ref,
                  cg_ref, cb_ref, g1_ref, b1_ref, o_ref, *, alpha):
    c = _layer_norm(conv_ref[...], cg_ref[...], cb_ref[...])
    c = c * jax.nn.sigmoid(c)
    conv_out = _dot(c.astype(BF16), wpc_ref[...])
    attn_out = _dot(attn_ref[...].astype(BF16), wpa_ref[...])
    mix = ga_ref[...] * attn_out + gc_ref[...] * conv_out
    mix = _dot(mix.astype(BF16), wo_ref[...])
    o_ref[...] = _layer_norm(alpha * x_ref[...] + mix, g1_ref[...], b1_ref[...])


def _merge(x, attn, conv, ga, gc, lw, alpha, tm):
    t, d = x.shape
    rows = lambda a: pl.BlockSpec((tm, a.shape[1]), lambda i: (i, 0))
    full = lambda a: pl.BlockSpec(a.shape, lambda i: (0,) * a.ndim)
    weights = (lw['w_pa'], lw['w_pc'], lw['w_o'], lw['conv_ln_g'], lw['conv_ln_b'], lw['ln1_g'], lw['ln1_b'])
    acts = (x, attn, conv, ga, gc)
    return pl.pallas_call(
        functools.partial(_merge_kernel, alpha=alpha),
        grid=(t // tm,),
        in_specs=[rows(a) for a in acts] + [full(w) for w in weights],
        out_specs=pl.BlockSpec((tm, d), lambda i: (i, 0)),
        out_shape=jax.ShapeDtypeStruct((t, d), F32),
        compiler_params=_cparams("parallel"),
        name="merge",
    )(*acts, *weights)


EXPERTS_PER_STEP = 2


def _route(x, wr_hi_ref, wr_lo_ref, rbias_ref):
    tm = x.shape[0]
    x_hi = x.astype(BF16)
    x_lo = (x - x_hi.astype(F32)).astype(BF16)
    logits = _dot_nt(wr_hi_ref[...], x_hi) + (_dot_nt(wr_hi_ref[...], x_lo) + _dot_nt(wr_lo_ref[...], x_hi))
    scores = jax.nn.sigmoid(logits)
    sel = scores + rbias_ref[...]
    neg_inf = -jnp.inf
    sub8 = lax.broadcasted_iota(I32, (GROUP_SIZE, tm), 0).astype(F32)

    def take_first_max(cur, idx, n):
        m = jnp.max(cur, axis=0, keepdims=True)
        first = jnp.min(jnp.where(cur == m, idx, float(n)), axis=0, keepdims=True)
        return m, idx == first

    gs_rows = []
    for g in range(N_GROUPS):
        blk = sel[g * GROUP_SIZE:(g + 1) * GROUP_SIZE, :]
        m1, hit = take_first_max(blk, sub8, GROUP_SIZE)
        m2 = jnp.max(jnp.where(hit, neg_inf, blk), axis=0, keepdims=True)
        gs_rows.append(m1 + m2)
    gs = jnp.concatenate(gs_rows, axis=0)
    gsel = jnp.zeros((N_GROUPS, tm), jnp.bool_)
    for _ in range(TOPK_GROUPS):
        _, hit = take_first_max(gs, sub8, N_GROUPS)
        gsel = jnp.logical_or(gsel, hit)
        gs = jnp.where(hit, neg_inf, gs)
    gself = jnp.where(gsel, 1.0, 0.0)
    emask = jnp.concatenate(
        [jnp.broadcast_to(gself[g:g + 1, :], (GROUP_SIZE, tm)) for g in range(N_GROUPS)], axis=0) > 0.5
    cur = jnp.where(emask, sel, neg_inf)
    sub64 = lax.broadcasted_iota(I32, (N_EXPERTS, tm), 0).astype(F32)
    chosen = jnp.zeros((N_EXPERTS, tm), jnp.bool_)
    for _ in range(MOE_TOPK):
        _, hit = take_first_max(cur, sub64, N_EXPERTS)
        chosen = jnp.logical_or(chosen, hit)
        cur = jnp.where(hit, neg_inf, cur)
    w = jnp.where(chosen, scores, 0.0)
    w = w / jnp.sum(w, axis=0, keepdims=True) * ROUTED_SCALE
    return w.T


def _moe_kernel(x_ref, wrh_ref, wrl_ref, rb_ref, sg_ref, su_ref, sd_ref, eg_ref, eu_ref, ed_ref,
                g2_ref, b2_ref, o_ref, xb_ref, gate_ref, acc_ref, *, alpha):
    e = pl.program_id(1)

    def ffn(xb, wg, wu):
        hg = _dot(xb, wg)
        return hg * jax.nn.sigmoid(hg) * _dot(xb, wu)

    @pl.when(e == 0)
    def _():
        x = x_ref[...]
        xb = x.astype(BF16)
        xb_ref[...] = xb
        gate_ref[...] = _route(x, wrh_ref, wrl_ref, rb_ref)
        acc_ref[...] = _dot(ffn(xb, sg_ref[...], su_ref[...]).astype(BF16), sd_ref[...])

    xb = xb_ref[...]
    gate = gate_ref[...]
    lane = lax.broadcasted_iota(I32, gate.shape, 1)
    for i in range(EXPERTS_PER_STEP):
        gcol = jnp.sum(jnp.where(lane == e * EXPERTS_PER_STEP + i, gate, 0.0), axis=1, keepdims=True)
        h = ffn(xb, eg_ref[0, i].astype(BF16), eu_ref[0, i].astype(BF16)) * gcol
        acc_ref[...] += _dot(h.astype(BF16), ed_ref[0, i].astype(BF16))

    @pl.when(e == pl.num_programs(1) - 1)
    def _():
        o_ref[...] = _layer_norm(alpha * x_ref[...] + acc_ref[...], g2_ref[...], b2_ref[...])


def _moe(x, lw, exp_w, layer, alpha, tm):
    t, d = x.shape
    full = lambda a: pl.BlockSpec(a.shape, lambda i, e: (0,) * a.ndim)
    exp_spec = lambda a: pl.BlockSpec((1, EXPERTS_PER_STEP) + a.shape[2:], lambda i, e: (layer, e, 0, 0))
    small_w = (lw['wr_hi'], lw['wr_lo'], lw['r_bias'], lw['w_sg'], lw['w_su'], lw['w_sd'])
    ln_w = (lw['ln2_g'], lw['ln2_b'])
    return pl.pallas_call(
        functools.partial(_moe_kernel, alpha=alpha),
        grid=(t // tm, N_EXPERTS // EXPERTS_PER_STEP),
        in_specs=[pl.BlockSpec((tm, d), lambda i, e: (i, 0))] + [full(w) for w in small_w]
                 + [exp_spec(w) for w in exp_w] + [full(w) for w in ln_w],
        out_specs=pl.BlockSpec((tm, d), lambda i, e: (i, 0)),
        out_shape=jax.ShapeDtypeStruct((t, d), F32),
        scratch_shapes=[pltpu.VMEM((tm, d), BF16), pltpu.VMEM((tm, N_EXPERTS), F32), pltpu.VMEM((tm, d), F32)],
        compiler_params=_cparams("parallel", "arbitrary"),
        name="moe",
    )(x, *small_w, *exp_w, *ln_w)


def _rope_tables(pos):
    rd = HEAD_DIM // 4
    half = rd // 2
    inv = ROPE_THETA ** (-jnp.arange(half, dtype=F32) * 2.0 / rd)
    ang = pos.astype(F32)[:, None] * inv[None, :]
    cos, sin = jnp.cos(ang), jnp.sin(ang)
    t = pos.shape[0]
    ones = jnp.ones((t, HEAD_DIM - rd), F32)
    zeros = jnp.zeros((t, HEAD_DIM - rd), F32)
    zh = jnp.zeros((t, half), F32)
    c64 = jnp.concatenate([cos, cos, ones], axis=1)
    a64 = jnp.concatenate([-sin, zh, zeros], axis=1)
    b64 = jnp.concatenate([zh, sin, zeros], axis=1)
    rep = lambda a: jnp.tile(a, (1, LANES // HEAD_DIM))
    return rep(c64), rep(a64), rep(b64)


def _split_hi_lo(w):
    hi = w.astype(BF16)
    return hi, (w - hi.astype(F32)).astype(BF16)


def _row_tile(t, want):
    return want if t % want == 0 else t


def kernel(x_prompt, x_sample, cache_k, cache_v, cache_idx_k, state_conv, page_table, w_in, b_in, idx_k_ln_g, idx_k_ln_b, conv_w, conv_b, conv_ln_g, conv_ln_b, w_pa, w_pc, w_o, ln1_g, ln1_b, w_router, router_bias, w_exp_gate, w_exp_up, w_exp_down, w_sh_gate, w_sh_up, w_sh_down, ln2_g, ln2_b):
    bsz, seq, d = x_prompt.shape
    bd, n_new, _ = x_sample.shape
    depth = w_in.shape[0]
    n_pool, page = cache_k.shape[1], cache_k.shape[2]
    past = page_table.shape[1] * page
    alpha = (2 * depth) ** 0.25
    assert seq % KEY_CHUNK == 0 or seq < KEY_CHUNK
    assert page == LANES and page_table.shape[1] % PAGES_PER_STEP == 0 and n_new <= SAMPLE_ROWS

    c_a = 4 * ATTN_W
    c_s = c_a + IDX_DIM + N_IDX_HEADS
    c_g = c_s + 2 * CONV_CH
    pad_s = LANES - (IDX_DIM + N_IDX_HEADS)
    row2 = lambda a: a.reshape(1, -1)

    cache_kt = cache_k.transpose(0, 1, 3, 4, 2)
    cache_vt = cache_v.transpose(0, 1, 3, 4, 2)
    cache_idx_kt = cache_idx_k.transpose(0, 1, 3, 2)
    tables_p = _rope_tables(jnp.arange(seq))
    tables_p = tuple(jnp.tile(a, (bsz, 1)) for a in tables_p)
    tables_s = _rope_tables(past + jnp.arange(n_new))
    tables_s = tuple(jnp.tile(a, (bd, 1)) for a in tables_s)
    state_t = state_conv.transpose(0, 2, 1, 3)

    xp = x_prompt.reshape(bsz * seq, d)
    xs = x_sample.reshape(bd * n_new, d)
    outs = {name: [] for name in ('kp', 'vp', 'kip', 'cp', 'ks', 'vs', 'kis', 'cs')}
    tm_p = _row_tile(bsz * seq, 256)
    tm_s = bd * n_new
    tm_moe = _row_tile(bsz * seq, 1024)
    exp_w = (w_exp_gate, w_exp_up, w_exp_down)

    for l in range(depth):
        wr_hi, wr_lo = _split_hi_lo(w_router[l].T)
        lw = {
            'w_a': w_in[l, :, :c_a].astype(BF16),
            'w_s': jnp.pad(w_in[l, :, c_a:c_s], ((0, 0), (0, pad_s))).astype(BF16),
            'w_g': w_in[l, :, c_s:c_g].astype(BF16),
            'w_t': w_in[l, :, c_g:].astype(BF16),
            'b_a': row2(b_in[l, :c_a]),
            'b_s': row2(jnp.pad(b_in[l, c_a:c_s], (0, pad_s))),
            'b_g': row2(b_in[l, c_s:c_g]),
            'b_t': row2(b_in[l, c_g:]),
            'idx_ln_g': row2(jnp.pad(idx_k_ln_g[l], (0, LANES - IDX_DIM))),
            'idx_ln_b': row2(jnp.pad(idx_k_ln_b[l], (0, LANES - IDX_DIM))),
            'w_pa': w_pa[l].astype(BF16), 'w_pc': w_pc[l].astype(BF16), 'w_o': w_o[l].astype(BF16),
            'conv_ln_g': row2(conv_ln_g[l]), 'conv_ln_b': row2(conv_ln_b[l]),
            'ln1_g': row2(ln1_g[l]), 'ln1_b': row2(ln1_b[l]),
            'wr_hi': wr_hi, 'wr_lo': wr_lo, 'r_bias': router_bias[l].reshape(-1, 1),
            'w_sg': w_sh_gate[l].astype(BF16), 'w_su': w_sh_up[l].astype(BF16), 'w_sd': w_sh_down[l].astype(BF16),
            'ln2_g': row2(ln2_g[l]), 'ln2_b': row2(ln2_b[l]),
        }
        cw = jnp.pad(conv_w[l], ((0, 1), (0, 0)))
        cb = row2(conv_b[l])

        q, kf, vf, vb, qi, small, u, ga, gc, kt, kit = _inproj(xp, lw, tables_p, tm_p)
        r3 = lambda a: a.reshape(bsz, seq, a.shape[-1])
        attn = _prompt_attn(r3(qi), r3(small), r3(q), kit, kt, r3(vb))
        u3 = r3(u)
        conv = _conv_prompt(u3, cw, cb, min(512, seq))
        outs['kp'].append(kf.reshape(bsz, seq, N_HEADS, HEAD_DIM))
        outs['vp'].append(vf.reshape(bsz, seq, N_HEADS, HEAD_DIM))
        outs['kip'].append(r3(small)[:, :, :IDX_DIM])
        outs['cp'].append(u3[:, seq - (CONV_W - 1):])
        x1 = _merge(xp, attn.reshape(bsz * seq, ATTN_W), conv.reshape(bsz * seq, CONV_CH), ga, gc, lw, alpha, tm_p)
        xp = _moe(x1, lw, exp_w, l, alpha, tm_moe)

        q, kf, vf, vb, qi, small, u, ga, gc, kt, kit = _inproj(xs, lw, tables_s, tm_s)
        s3 = lambda a: a.reshape(bd, n_new, a.shape[-1])
        attn = _sample_attention(l, page_table, s3(q), s3(qi), s3(small), kit[:IDX_DIM], kt, s3(vb),
                                 cache_kt, cache_vt, cache_idx_kt)
        u3 = s3(u)
        conv = _conv_sample(state_t[l], u3.transpose(1, 0, 2), cw, cb).transpose(1, 0, 2)
        outs['ks'].append(kf.reshape(bd, n_new, N_HEADS, HEAD_DIM))
        outs['vs'].append(vf.reshape(bd, n_new, N_HEADS, HEAD_DIM))
        outs['kis'].append(s3(small)[:, :, :IDX_DIM])
        outs['cs'].append(jnp.concatenate([state_conv[l][:, n_new:], u3], axis=1))
        x1 = _merge(xs, attn.reshape(bd * n_new, ATTN_W), conv.reshape(bd * n_new, CONV_CH), ga, gc, lw, alpha, tm_s)
        xs = _moe(x1, lw, exp_w, l, alpha, tm_s)

    st = lambda name: jnp.stack(outs[name])
    return (xp.reshape(bsz, seq, d), xs.reshape(bd, n_new, d), st('kp'), st('vp'), st('kip'), st('cp'),
            st('ks'), st('vs'), st('kis'), st('cs'))
```

```python
import functools
import math

import jax
import jax.numpy as jnp
from jax import lax
from jax.experimental import pallas as pl
from jax.experimental.pallas import tpu as pltpu

F32 = jnp.float32
BF16 = jnp.bfloat16
I32 = jnp.int32

N_HEADS = 8
HEAD_DIM = 64
ATTN_W = N_HEADS * HEAD_DIM
N_IDX_HEADS = 8
IDX_DIM = 64
IDX_W_SCALE = (N_IDX_HEADS * IDX_DIM) ** -0.5
TOPK_MAX = 256
ROPE_THETA = 500000.0
CONV_CH = 512
CONV_W = 31
N_EXPERTS = 64
EXPERT_FF = 256
MOE_TOPK = 8
N_GROUPS = 8
GROUP_SIZE = N_EXPERTS // N_GROUPS
TOPK_GROUPS = 4
ROUTED_SCALE = 2.5
LN_EPS = 1e-5

LANES = 128
Q_ROWS = 128
KEY_CHUNK = 512
ATTN_CHUNK = 256
INT_MIN = -(2 ** 31)
INT_MAX = 2 ** 31 - 1
BOUND_CLASSES = 256
NEG_BIG = -1e30
VMEM_LIMIT = 56 * 1024 * 1024


def _cparams(*sem):
    return pltpu.CompilerParams(dimension_semantics=sem, vmem_limit_bytes=VMEM_LIMIT)


def _layer_norm(x, g, b):
    mu = jnp.mean(x, axis=-1, keepdims=True)
    xc = x - mu
    var = jnp.mean(xc * xc, axis=-1, keepdims=True)
    return xc * lax.rsqrt(var + LN_EPS) * g + b


def _dot(a, b):
    return jnp.dot(a, b, preferred_element_type=F32)


def _dot_nt(a, b):
    return lax.dot_general(a, b, (((1,), (1,)), ((), ())), preferred_element_type=F32)


def _inproj_kernel(x_ref, wa_ref, ws_ref, wg_ref, wt_ref, ba_ref, bs_ref, bg_ref, bt_ref,
                   lng_ref, lnb_ref, cos_ref, s1_ref, s2_ref,
                   q_ref, kf_ref, vf_ref, vb_ref, qi_ref, small_ref,
                   u_ref, ga_ref, gc_ref, kt_ref, kit_ref):
    xb = x_ref[...].astype(BF16)
    cos = cos_ref[...]
    s1 = s1_ref[...]
    s2 = s2_ref[...]

    def rope128(v, c, a, b):
        return v * c + pltpu.roll(v, LANES - 8, 1) * a + pltpu.roll(v, 8, 1) * b

    def rope_seg(seg):
        return jnp.concatenate(
            [rope128(seg[:, c * LANES:(c + 1) * LANES], cos, s1, s2) for c in range(ATTN_W // LANES)], axis=1)

    ha = _dot(xb, wa_ref[...]) + ba_ref[...]
    q = rope_seg(ha[:, 0:ATTN_W])
    k = rope_seg(ha[:, ATTN_W:2 * ATTN_W])
    v = ha[:, 2 * ATTN_W:3 * ATTN_W]
    qi = rope_seg(ha[:, 3 * ATTN_W:4 * ATTN_W])
    q_ref[...] = (q * (HEAD_DIM ** -0.5)).astype(BF16)
    kf_ref[...] = k
    kt_ref[...] = k.T.astype(BF16)
    vf_ref[...] = v
    vb_ref[...] = v.astype(BF16)
    qi_ref[...] = qi.astype(BF16)

    hs = _dot(xb, ws_ref[...]) + bs_ref[...]
    lane = lax.broadcasted_iota(I32, hs.shape, 1)
    is_key = lane < IDX_DIM
    mu = jnp.sum(jnp.where(is_key, hs, 0.0), axis=1, keepdims=True) * (1.0 / IDX_DIM)
    hc = jnp.where(is_key, hs - mu, 0.0)
    var = jnp.sum(hc * hc, axis=1, keepdims=True) * (1.0 / IDX_DIM)
    kin = hc * lax.rsqrt(var + LN_EPS) * lng_ref[...] + lnb_ref[...]
    kir = rope128(kin, jnp.where(is_key, cos, 1.0), jnp.where(is_key, s1, 0.0), jnp.where(is_key, s2, 0.0))
    small = jnp.where(is_key, kir, jnp.where(lane < IDX_DIM + N_IDX_HEADS, hs * IDX_W_SCALE, 0.0))
    small_ref[...] = small
    kit = small.T[0:IDX_DIM, :].astype(BF16)
    kit_ref[...] = jnp.concatenate([kit, kit], axis=0)

    hg = _dot(xb, wg_ref[...]) + bg_ref[...]
    u_ref[...] = hg[:, 0:CONV_CH] * jax.nn.sigmoid(hg[:, CONV_CH:2 * CONV_CH])

    d = ga_ref.shape[1]
    ht = jax.nn.sigmoid(_dot(xb, wt_ref[...]) + bt_ref[...])
    ga_ref[...] = ht[:, 0:d]
    gc_ref[...] = ht[:, d:2 * d]


def _inproj(x, lw, tables, tm):
    t, d = x.shape
    cos, s1, s2 = tables
    full = lambda a: pl.BlockSpec(a.shape, lambda i: (0,) * a.ndim)
    rows = lambda w: pl.BlockSpec((tm, w), lambda i: (i, 0))
    weights = (lw['w_a'], lw['w_s'], lw['w_g'], lw['w_t'], lw['b_a'], lw['b_s'], lw['b_g'], lw['b_t'],
               lw['idx_ln_g'], lw['idx_ln_b'])
    out_shapes = (
        jax.ShapeDtypeStruct((t, ATTN_W), BF16),
        jax.ShapeDtypeStruct((t, ATTN_W), F32),
        jax.ShapeDtypeStruct((t, ATTN_W), F32),
        jax.ShapeDtypeStruct((t, ATTN_W), BF16),
        jax.ShapeDtypeStruct((t, ATTN_W), BF16),
        jax.ShapeDtypeStruct((t, LANES), F32),
        jax.ShapeDtypeStruct((t, CONV_CH), F32),
        jax.ShapeDtypeStruct((t, d), F32),
        jax.ShapeDtypeStruct((t, d), F32),
    )
    cols_shapes = (
        jax.ShapeDtypeStruct((ATTN_W, t), BF16),
        jax.ShapeDtypeStruct((LANES, t), BF16),
    )
    cols = lambda h: pl.BlockSpec((h, tm), lambda i: (0, i))
    return pl.pallas_call(
        _inproj_kernel,
        grid=(t // tm,),
        in_specs=[rows(d)] + [full(w) for w in weights] + [rows(LANES)] * 3,
        out_specs=tuple(rows(s.shape[1]) for s in out_shapes) + tuple(cols(s.shape[0]) for s in cols_shapes),
        out_shape=out_shapes + cols_shapes,
        compiler_params=_cparams("parallel"),
        name="inproj",
    )(x, *weights, cos, s1, s2)


def _sort_key(score, admissible):
    bits = pltpu.bitcast(score, I32)
    bits = jnp.where(score == 0.0, 0, bits)
    key = jnp.where(bits >= 0, bits, bits ^ 0x7FFFFFFF)
    return jnp.where(admissible, key, INT_MIN)


def _fold_lanes(x):
    w = x.shape[1]
    acc = x[:, 0:LANES]
    for c in range(1, w // LANES):
        acc = acc + x[:, c * LANES:(c + 1) * LANES]
    return acc


def _select_threshold(keys_ref, rows, chunk, n_chunks, ksel, n_cols, n_adm=None):
    lane = lax.broadcasted_iota(I32, (rows, chunk), 1)

    def count(pred):
        def body(c, cnt):
            off = pl.multiple_of(c * chunk, chunk)
            kc = keys_ref[:, pl.ds(off, chunk)]
            return cnt + _fold_lanes(jnp.where(pred(kc, off + lane), 1, 0))
        cnt = lax.fori_loop(0, n_chunks, body, jnp.zeros((rows, LANES), I32))
        return jnp.sum(cnt.astype(F32), axis=1, keepdims=True).astype(I32)

    if chunk % BOUND_CLASSES == 0:
        def bound_body(c, g):
            off = pl.multiple_of(c * chunk, chunk)
            kc = keys_ref[:, pl.ds(off, chunk)]
            for s in range(chunk // BOUND_CLASSES):
                g = jnp.maximum(g, kc[:, s * BOUND_CLASSES:(s + 1) * BOUND_CLASSES])
            return g
        g = lax.fori_loop(0, n_chunks, bound_body, jnp.full((rows, BOUND_CLASSES), INT_MIN, I32))
        g_hi, g_lo = g[:, 0:LANES], g[:, 0:LANES]
        for s in range(1, BOUND_CLASSES // LANES):
            g_hi = jnp.maximum(g_hi, g[:, s * LANES:(s + 1) * LANES])
            g_lo = jnp.minimum(g_lo, g[:, s * LANES:(s + 1) * LANES])
        hi0 = jnp.max(g_hi, axis=1, keepdims=True)
        lo0 = jnp.min(g_lo, axis=1, keepdims=True)
    else:
        hi0 = jnp.full((rows, 1), INT_MAX, I32)
        lo0 = jnp.full((rows, 1), INT_MIN, I32)

    if n_adm is None:
        n_adm = count(lambda kc, cols: kc > INT_MIN)
    small = n_adm < ksel
    lo0 = jnp.where(small, INT_MIN, lo0)
    hi0 = jnp.where(small, INT_MIN, hi0)

    width_bits = 32 - lax.clz(hi0 - lo0)
    n_steps = jnp.max(width_bits.astype(F32)).astype(I32)

    def bisect_body(i, state):
        lo, hi = state
        mid = (lo >> 1) + (hi >> 1) + ((lo | hi) & 1)
        ge = count(lambda kc, cols: kc >= mid) >= ksel
        return jnp.where(ge, mid, lo), jnp.where(ge, hi, mid - 1)

    thr, _ = lax.fori_loop(0, n_steps, bisect_body, (lo0, hi0))

    cnt_gt = count(lambda kc, cols: kc > thr)
    cnt_ge = count(lambda kc, cols: kc >= thr)
    need = ksel - cnt_gt
    live = thr != INT_MIN
    over = jnp.logical_and(live, cnt_ge > ksel)
    n_bits = max(1, (n_cols - 1).bit_length())

    def tie_search():
        def body(i, m):
            cand = m | lax.shift_left(jnp.int32(1), n_bits - 1 - i)
            g = count(lambda kc, cols: jnp.logical_and(kc == thr, cols < cand))
            return jnp.where(g < need, cand, m)
        return lax.fori_loop(0, n_bits, body, jnp.zeros((rows, 1), I32))

    any_over = jnp.max(jnp.where(over, 1.0, 0.0)) > 0.5
    m = lax.cond(any_over, tie_search, lambda: jnp.zeros((rows, 1), I32))
    cut = jnp.where(live, jnp.where(over, m, n_cols), -1).astype(I32)
    return thr, cut


def _selected(kc, cols, thr, cut):
    return jnp.logical_or(kc > thr, jnp.logical_and(kc == thr, cols <= cut))


def _head_masked(x):
    lane = lax.broadcasted_iota(I32, (x.shape[0], LANES), 1)
    out = []
    for h in range(N_HEADS):
        pair = x[:, (h // 2) * LANES:(h // 2 + 1) * LANES]
        keep = (lane < HEAD_DIM) if h % 2 == 0 else (lane >= HEAD_DIM)
        out.append(jnp.where(keep, pair, jnp.zeros_like(pair)))
    return out


def _prompt_attn_kernel(qi_ref, small_ref, q_ref, kit_ref, kt_ref, v_ref, o_ref, keys_ref, m_ref, l_ref, acc_ref,
                        *, seq, chunk, achunk, ksel):
    j = pl.program_id(1)
    n_chunks = (j * Q_ROWS + Q_ROWS + chunk - 1) // chunk
    wi = small_ref[0]
    row = j * Q_ROWS + lax.broadcasted_iota(I32, (Q_ROWS, chunk), 0)
    lane = lax.broadcasted_iota(I32, (Q_ROWS, chunk), 1)
    wcols = [wi[:, IDX_DIM + h:IDX_DIM + h + 1] for h in range(N_IDX_HEADS)]
    qi_h = _head_masked(qi_ref[0])

    def idx_body(c, carry):
        off = pl.multiple_of(c * chunk, chunk)
        kic = kit_ref[:, pl.ds(off, chunk)]
        score = jnp.zeros((Q_ROWS, chunk), F32)
        for h in range(N_IDX_HEADS):
            score = score + jnp.maximum(_dot(qi_h[h], kic), 0.0) * wcols[h]
        keys_ref[:, pl.ds(off, chunk)] = _sort_key(score, (off + lane) <= row)
        return carry

    lax.fori_loop(0, n_chunks, idx_body, 0)
    n_adm = jnp.max(row[:, 0:LANES], axis=1, keepdims=True) + 1
    thr, cut = _select_threshold(keys_ref, Q_ROWS, chunk, n_chunks, ksel, seq, n_adm)

    q_h = _head_masked(q_ref[0])
    n_ach = (j * Q_ROWS + Q_ROWS + achunk - 1) // achunk
    lane_a = lax.broadcasted_iota(I32, (Q_ROWS, achunk), 1)
    even = lax.broadcasted_iota(I32, (Q_ROWS, LANES), 1) < HEAD_DIM
    ones = jnp.ones((achunk, LANES), BF16)
    m_ref[...] = jnp.full(m_ref.shape, NEG_BIG, F32)
    l_ref[...] = jnp.zeros_like(l_ref)
    acc_ref[...] = jnp.zeros_like(acc_ref)

    def attn_body(c, carry):
        off = pl.multiple_of(c * achunk, achunk)
        kc = keys_ref[:, pl.ds(off, achunk)]
        bias = jnp.where(_selected(kc, off + lane_a, thr, cut), 0.0, NEG_BIG)
        for pr in range(N_HEADS // 2):
            sl = slice(pr * LANES, (pr + 1) * LANES)
            kk = kt_ref[sl, pl.ds(off, achunk)]
            vv = jnp.concatenate([v_ref[0, pl.ds(off, achunk), sl], ones], axis=1)
            pvs, alphas = [], []
            for h in (2 * pr, 2 * pr + 1):
                hl = slice(h * LANES, (h + 1) * LANES)
                s = _dot(q_h[h], kk) + bias
                m_old = m_ref[:, hl]
                m_new = jnp.maximum(m_old, jnp.max(s, axis=1, keepdims=True))
                m_ref[:, hl] = m_new
                alpha = jnp.exp(m_old - m_new)
                p = jnp.exp(s - jnp.concatenate([m_new] * (achunk // LANES), axis=1))
                pv = _dot(p.astype(BF16), vv)
                l_ref[:, hl] = alpha * l_ref[:, hl] + pv[:, LANES:]
                pvs.append(pv[:, :LANES])
                alphas.append(alpha)
            acc_ref[:, sl] = (jnp.where(even, alphas[0], alphas[1]) * acc_ref[:, sl]
                              + jnp.where(even, pvs[0], pvs[1]))
        return carry

    lax.fori_loop(0, n_ach, attn_body, 0)
    for pr in range(N_HEADS // 2):
        sl = slice(pr * LANES, (pr + 1) * LANES)
        l_pair = jnp.where(even, l_ref[:, 2 * pr * LANES:(2 * pr + 1) * LANES],
                           l_ref[:, (2 * pr + 1) * LANES:(2 * pr + 2) * LANES])
        o_ref[0, :, sl] = (acc_ref[:, sl] / l_pair).astype(o_ref.dtype)


def _prompt_attn(qi, small, q, kit, kt, vb):
    b, s, _ = q.shape
    chunk = min(KEY_CHUNK, s)
    achunk = min(ATTN_CHUNK, s)
    ksel = min(TOPK_MAX, s // 4)
    qblk = lambda w: pl.BlockSpec((1, Q_ROWS, w), lambda bi, j: (bi, j, 0))
    return pl.pallas_call(
        functools.partial(_prompt_attn_kernel, seq=s, chunk=chunk, achunk=achunk, ksel=ksel),
        grid=(b, s // Q_ROWS),
        in_specs=[qblk(ATTN_W), qblk(LANES), qblk(ATTN_W),
                  pl.BlockSpec((LANES, s), lambda bi, j: (0, bi)),
                  pl.BlockSpec((ATTN_W, s), lambda bi, j: (0, bi)),
                  pl.BlockSpec((1, s, ATTN_W), lambda bi, j: (bi, 0, 0))],
        out_specs=qblk(ATTN_W),
        out_shape=jax.ShapeDtypeStruct((b, s, ATTN_W), BF16),
        scratch_shapes=[pltpu.VMEM((Q_ROWS, s), I32), pltpu.VMEM((Q_ROWS, N_HEADS * LANES), F32),
                        pltpu.VMEM((Q_ROWS, N_HEADS * LANES), F32), pltpu.VMEM((Q_ROWS, ATTN_W), F32)],
        compiler_params=_cparams("parallel", "arbitrary"),
        name="prompt_attn",
    )(qi, small, q, kit, kt, vb)


SAMPLE_ROWS = 8
PAGES_PER_STEP = 8
HEAD_ROWS = N_HEADS * SAMPLE_ROWS


def _sample_scores_kernel(pt_ref, qi_ref, wcol_ref, kin_ref, *rest, page, past, n_new):
    pages = rest[:PAGES_PER_STEP]
    keys_ref = rest[PAGES_PER_STEP]
    st = pl.program_id(1)
    n_steps = pl.num_programs(1)
    qi = qi_ref[0]
    wcol = wcol_ref[0]
    lane = lax.broadcasted_iota(I32, (SAMPLE_ROWS, page), 1)
    row = lax.broadcasted_iota(I32, (SAMPLE_ROWS, page), 0)

    def scores(kit_bf16):
        d = jnp.maximum(_dot(qi, kit_bf16), 0.0) * wcol
        sc = d[0:SAMPLE_ROWS]
        for h in range(1, N_IDX_HEADS):
            sc = sc + d[h * SAMPLE_ROWS:(h + 1) * SAMPLE_ROWS]
        return sc

    for p in range(PAGES_PER_STEP):
        off = pl.multiple_of((st * PAGES_PER_STEP + p) * page, page)
        keys_ref[:, pl.ds(off, page)] = _sort_key(scores(pages[p][0, 0].astype(BF16)), row < n_new)

    @pl.when(st == n_steps - 1)
    def _():
        adm = jnp.logical_and(row < n_new, lane <= row)
        keys_ref[:, pl.ds(past, page)] = _sort_key(scores(kin_ref[0]), adm)


def _sample_select_kernel(keys_ref, bias_ref, *, chunk, ksel, n_cols):
    rows = keys_ref.shape[0]
    n_chunks = n_cols // chunk
    thr, cut = _select_threshold(keys_ref, rows, chunk, n_chunks, ksel, n_cols)
    lane = lax.broadcasted_iota(I32, (rows, chunk), 1)

    def body(c, carry):
        off = pl.multiple_of(c * chunk, chunk)
        kc = keys_ref[:, pl.ds(off, chunk)]
        bias_ref[:, pl.ds(off, chunk)] = jnp.where(_selected(kc, off + lane, thr, cut), 0.0, NEG_BIG)
        return carry

    lax.fori_loop(0, n_chunks, body, 0)


def _sample_attn_kernel(pt_ref, q_ref, bias_ref, knew_ref, vnew_ref, *rest, page, past):
    kpages = rest[:PAGES_PER_STEP]
    vpages = rest[PAGES_PER_STEP:2 * PAGES_PER_STEP]
    o_ref = rest[2 * PAGES_PER_STEP]
    m_ref, l_ref, acc_ref = rest[2 * PAGES_PER_STEP + 1:]
    st = pl.program_id(1)
    n_steps = pl.num_programs(1)
    q = q_ref[0]

    @pl.when(st == 0)
    def _():
        m_ref[...] = jnp.full_like(m_ref, NEG_BIG)
        l_ref[...] = jnp.zeros_like(l_ref)
        acc_ref[...] = jnp.zeros_like(acc_ref)

    def scores(kpage_t, bias_tok):
        return _dot(q, kpage_t) + jnp.concatenate([bias_tok] * N_HEADS, axis=0)

    def update(s_list, v_list):
        m_old = m_ref[...]
        m_new = m_old
        for s in s_list:
            m_new = jnp.maximum(m_new, jnp.max(s, axis=1, keepdims=True))
        alpha = jnp.exp(m_old - m_new)
        l_new = alpha * l_ref[...]
        acc = alpha * acc_ref[...]
        for s, vpage_t in zip(s_list, v_list):
            pp = jnp.exp(s - m_new)
            l_new = l_new + jnp.sum(pp, axis=1, keepdims=True)
            acc = acc + _dot_nt(pp.astype(BF16), vpage_t)
        m_ref[...] = m_new
        l_ref[...] = l_new
        acc_ref[...] = acc

    as_rows = lambda ref: ref[0, 0].reshape(ATTN_W, page).astype(BF16)
    s_list = []
    for p in range(PAGES_PER_STEP):
        off = pl.multiple_of((st * PAGES_PER_STEP + p) * page, page)
        s_list.append(scores(as_rows(kpages[p]), bias_ref[0, :, pl.ds(off, page)]))
    update(s_list, [as_rows(vpages[p]) for p in range(PAGES_PER_STEP)])

    @pl.when(st == n_steps - 1)
    def _():
        update([scores(knew_ref[0], bias_ref[0, :, past:past + page])], [vnew_ref[0]])
        out = acc_ref[...] / l_ref[...]
        for h in range(N_HEADS):
            hs = slice(h * HEAD_DIM, (h + 1) * HEAD_DIM)
            o_ref[0, :, hs] = out[h * SAMPLE_ROWS:(h + 1) * SAMPLE_ROWS, hs]


def _sample_attention(layer, page_table, q, qi, small, kit_new, kt_new, vb_new, cache_kt, cache_vt, cache_idx_kt):
    bd, n_new, _ = q.shape
    n_pages = page_table.shape[1]
    page = cache_idx_kt.shape[3]
    past = n_pages * page
    ksel = min(TOPK_MAX, (past + n_new) // 4)
    n_steps = n_pages // PAGES_PER_STEP
    n_cols = past + page

    pad_tok = lambda a: jnp.pad(a, ((0, 0), (0, SAMPLE_ROWS - n_new), (0, 0)))
    head_major = lambda a: pad_tok(a).reshape(bd, SAMPLE_ROWS, N_HEADS, -1).transpose(0, 2, 1, 3)
    qi_rows = head_major(qi).reshape(bd, HEAD_ROWS, IDX_DIM)
    wcol = head_major(small[:, :, IDX_DIM:IDX_DIM + N_IDX_HEADS]).reshape(bd, HEAD_ROWS, 1)
    qh = head_major(q)
    eye = jnp.eye(N_HEADS, dtype=q.dtype)
    q_rows = (qh[:, :, :, None, :] * eye[None, :, None, :, None]).reshape(bd, HEAD_ROWS, ATTN_W)
    new_page = lambda a_t: jnp.pad(a_t.reshape(-1, bd, n_new).transpose(1, 0, 2), ((0, 0), (0, 0), (0, page - n_new)))
    kin_pad, knew_pad = new_page(kit_new), new_page(kt_new)
    vnew_pad = jnp.pad(vb_new.transpose(0, 2, 1), ((0, 0), (0, 0), (0, page - n_new)))

    def kv_page_spec(p):
        return pl.BlockSpec((1, 1, N_HEADS, HEAD_DIM, page),
                            lambda b, st, pt, p=p: (layer, pt[b, st * PAGES_PER_STEP + p], 0, 0, 0))

    def idx_page_spec(p):
        return pl.BlockSpec((1, 1, IDX_DIM, page),
                            lambda b, st, pt, p=p: (layer, pt[b, st * PAGES_PER_STEP + p], 0, 0))

    per_b = lambda shape: pl.BlockSpec((1,) + shape, lambda b, st, pt: (b, 0, 0))

    keys = pl.pallas_call(
        functools.partial(_sample_scores_kernel, page=page, past=past, n_new=n_new),
        grid_spec=pltpu.PrefetchScalarGridSpec(
            num_scalar_prefetch=1,
            grid=(bd, n_steps),
            in_specs=[per_b((HEAD_ROWS, IDX_DIM)), per_b((HEAD_ROWS, 1)), per_b((IDX_DIM, page))]
                     + [idx_page_spec(p) for p in range(PAGES_PER_STEP)],
            out_specs=pl.BlockSpec((SAMPLE_ROWS, n_cols), lambda b, st, pt: (b, 0)),
        ),
        out_shape=jax.ShapeDtypeStruct((bd * SAMPLE_ROWS, n_cols), I32),
        compiler_params=_cparams("parallel", "arbitrary"),
        name="sample_scores",
    )(page_table, qi_rows, wcol, kin_pad, *([cache_idx_kt] * PAGES_PER_STEP))

    sel_rows = min(Q_ROWS, bd * SAMPLE_ROWS)
    bias = pl.pallas_call(
        functools.partial(_sample_select_kernel, chunk=page, ksel=ksel, n_cols=n_cols),
        grid=(bd * SAMPLE_ROWS // sel_rows,),
        in_specs=[pl.BlockSpec((sel_rows, n_cols), lambda i: (i, 0))],
        out_specs=pl.BlockSpec((sel_rows, n_cols), lambda i: (i, 0)),
        out_shape=jax.ShapeDtypeStruct((bd * SAMPLE_ROWS, n_cols), F32),
        compiler_params=_cparams("parallel"),
        name="sample_select",
    )(keys).reshape(bd, SAMPLE_ROWS, n_cols)

    out = pl.pallas_call(
        functools.partial(_sample_attn_kernel, page=page, past=past),
        grid_spec=pltpu.PrefetchScalarGridSpec(
            num_scalar_prefetch=1,
            grid=(bd, n_steps),
            in_specs=[per_b((HEAD_ROWS, ATTN_W)), per_b((SAMPLE_ROWS, n_cols)), per_b((ATTN_W, page)),
                      per_b((ATTN_W, page))] + [kv_page_spec(p) for p in range(PAGES_PER_STEP)] * 2,
            out_specs=per_b((SAMPLE_ROWS, ATTN_W)),
            scratch_shapes=[pltpu.VMEM((HEAD_ROWS, 1), F32), pltpu.VMEM((HEAD_ROWS, 1), F32),
                            pltpu.VMEM((HEAD_ROWS, ATTN_W), F32)],
        ),
        out_shape=jax.ShapeDtypeStruct((bd, SAMPLE_ROWS, ATTN_W), F32),
        compiler_params=_cparams("parallel", "arbitrary"),
        name="sample_attn",
    )(page_table, q_rows, bias, knew_pad, vnew_pad, *([cache_kt] * PAGES_PER_STEP), *([cache_vt] * PAGES_PER_STEP))
    return out[:, :n_new]


CONV_HALO = 32


def _conv_prompt_kernel(u_ref, prev_ref, w_ref, b_ref, o_ref, ext_ref, *, tile):
    i = pl.program_id(1)
    halo = prev_ref[0, tile - CONV_HALO:tile, :]
    ext_ref[0:CONV_HALO, :] = jnp.where(i > 0, halo, 0.0)
    ext_ref[CONV_HALO:CONV_HALO + tile, :] = u_ref[0]
    acc = jnp.broadcast_to(b_ref[...], (tile, CONV_CH))
    base = CONV_HALO - (CONV_W - 1)
    for k in range(CONV_W):
        acc = acc + ext_ref[base + k:base + k + tile, :] * w_ref[k:k + 1, :]
    o_ref[0] = acc


def _conv_prompt(u, w, b, tile):
    bsz, s, c = u.shape
    return pl.pallas_call(
        functools.partial(_conv_prompt_kernel, tile=tile),
        grid=(bsz, s // tile),
        in_specs=[pl.BlockSpec((1, tile, c), lambda bi, i: (bi, i, 0)),
                  pl.BlockSpec((1, tile, c), lambda bi, i: (bi, jnp.maximum(i - 1, 0), 0)),
                  pl.BlockSpec(w.shape, lambda bi, i: (0, 0)),
                  pl.BlockSpec(b.shape, lambda bi, i: (0, 0))],
        out_specs=pl.BlockSpec((1, tile, c), lambda bi, i: (bi, i, 0)),
        out_shape=jax.ShapeDtypeStruct((bsz, s, c), F32),
        scratch_shapes=[pltpu.VMEM((CONV_HALO + tile, c), F32)],
        compiler_params=_cparams("parallel", "arbitrary"),
        name="conv_prompt",
    )(u, u, w, b)


def _conv_sample_kernel(state_ref, u_ref, w_ref, b_ref, o_ref):
    n_hist = state_ref.shape[0]
    n_new = u_ref.shape[0]
    for t in range(n_new):
        acc = jnp.broadcast_to(b_ref[...], o_ref.shape[1:])
        for j in range(t, n_hist):
            acc = acc + state_ref[j] * w_ref[j - t:j - t + 1, :]
        for i in range(t + 1):
            k = n_hist - t + i
            acc = acc + u_ref[i] * w_ref[k:k + 1, :]
        o_ref[t] = acc


def _conv_sample(state_t, u_t, w, b):
    return pl.pallas_call(
        _conv_sample_kernel,
        out_shape=jax.ShapeDtypeStruct(u_t.shape, F32),
        name="conv_sample",
    )(state_t, u_t, w, b)


def _merge_kernel(x_ref, attn_ref, conv_ref, ga_ref, gc_ref, wpa_ref, wpc_ref, wo_ref,
                  cg_ref, cb_ref, g1_ref, b1_ref, o_ref, *, alpha):
    c = _layer_norm(conv_ref[...], cg_ref[...], cb_ref[...])
    c = c * jax.nn.sigmoid(c)
    conv_out = _dot(c.astype(BF16), wpc_ref[...])
    attn_out = _dot(attn_ref[...].astype(BF16), wpa_ref[...])
    mix = ga_ref[...] * attn_out + gc_ref[...] * conv_out
    mix = _dot(mix.astype(BF16), wo_ref[...])
    o_ref[...] = _layer_norm(alpha * x_ref[...] + mix, g1_ref[...], b1_ref[...])


def _merge(x, attn, conv, ga, gc, lw, alpha, tm):
    t, d = x.shape
    rows = lambda a: pl.BlockSpec((tm, a.shape[1]), lambda i: (i, 0))
    full = lambda a: pl.BlockSpec(a.shape, lambda i: (0,) * a.ndim)
    weights = (lw['w_pa'], lw['w_pc'], lw['w_o'], lw['conv_ln_g'], lw['conv_ln_b'], lw['ln1_g'], lw['ln1_b'])
    acts = (x, attn, conv, ga, gc)
    return pl.pallas_call(
        functools.partial(_merge_kernel, alpha=alpha),
        grid=(t // tm,),
        in_specs=[rows(a) for a in acts] + [full(w) for w in weights],
        out_specs=pl.BlockSpec((tm, d), lambda i: (i, 0)),
        out_shape=jax.ShapeDtypeStruct((t, d), F32),
        compiler_params=_cparams("parallel"),
        name="merge",
    )(*acts, *weights)


EXPERTS_PER_STEP = 2


def _route(x, wr_hi_ref, wr_lo_ref, rbias_ref):
    tm = x.shape[0]
    x_hi = x.astype(BF16)
    x_lo = (x - x_hi.astype(F32)).astype(BF16)
    logits = _dot_nt(wr_hi_ref[...], x_hi) + (_dot_nt(wr_hi_ref[...], x_lo) + _dot_nt(wr_lo_ref[...], x_hi))
    scores = jax.nn.sigmoid(logits)
    sel = scores + rbias_ref[...]
    neg_inf = -jnp.inf
    sub8 = lax.broadcasted_iota(I32, (GROUP_SIZE, tm), 0).astype(F32)

    def take_first_max(cur, idx, n):
        m = jnp.max(cur, axis=0, keepdims=True)
        first = jnp.min(jnp.where(cur == m, idx, float(n)), axis=0, keepdims=True)
        return m, idx == first

    gs_rows = []
    for g in range(N_GROUPS):
        blk = sel[g * GROUP_SIZE:(g + 1) * GROUP_SIZE, :]
        m1, hit = take_first_max(blk, sub8, GROUP_SIZE)
        m2 = jnp.max(jnp.where(hit, neg_inf, blk), axis=0, keepdims=True)
        gs_rows.append(m1 + m2)
    gs = jnp.concatenate(gs_rows, axis=0)
    gsel = jnp.zeros((N_GROUPS, tm), jnp.bool_)
    for _ in range(TOPK_GROUPS):
        _, hit = take_first_max(gs, sub8, N_GROUPS)
        gsel = jnp.logical_or(gsel, hit)
        gs = jnp.where(hit, neg_inf, gs)
    gself = jnp.where(gsel, 1.0, 0.0)
    emask = jnp.concatenate(
        [jnp.broadcast_to(gself[g:g + 1, :], (GROUP_SIZE, tm)) for g in range(N_GROUPS)], axis=0) > 0.5
    cur = jnp.where(emask, sel, neg_inf)
    sub64 = lax.broadcasted_iota(I32, (N_EXPERTS, tm), 0).astype(F32)
    chosen = jnp.zeros((N_EXPERTS, tm), jnp.bool_)
    for _ in range(MOE_TOPK):
        _, hit = take_first_max(cur, sub64, N_EXPERTS)
        chosen = jnp.logical_or(chosen, hit)
        cur = jnp.where(hit, neg_inf, cur)
    w = jnp.where(chosen, scores, 0.0)
    w = w / jnp.sum(w, axis=0, keepdims=True) * ROUTED_SCALE
    return w.T


def _moe_kernel(x_ref, wrh_ref, wrl_ref, rb_ref, sg_ref, su_ref, sd_ref, eg_ref, eu_ref, ed_ref,
                g2_ref, b2_ref, o_ref, xb_ref, gate_ref, acc_ref, *, alpha):
    e = pl.program_id(1)

    def ffn(xb, wg, wu):
        hg = _dot(xb, wg)
        return hg * jax.nn.sigmoid(hg) * _dot(xb, wu)

    @pl.when(e == 0)
    def _():
        x = x_ref[...]
        xb = x.astype(BF16)
        xb_ref[...] = xb
        gate_ref[...] = _route(x, wrh_ref, wrl_ref, rb_ref)
        acc_ref[...] = _dot(ffn(xb, sg_ref[...], su_ref[...]).astype(BF16), sd_ref[...])

    xb = xb_ref[...]
    gate = gate_ref[...]
    lane = lax.broadcasted_iota(I32, gate.shape, 1)
    for i in range(EXPERTS_PER_STEP):
        gcol = jnp.sum(jnp.where(lane == e * EXPERTS_PER_STEP + i, gate, 0.0), axis=1, keepdims=True)
        h = ffn(xb, eg_ref[0, i].astype(BF16), eu_ref[0, i].astype(BF16)) * gcol
        acc_ref[...] += _dot(h.astype(BF16), ed_ref[0, i].astype(BF16))

    @pl.when(e == pl.num_programs(1) - 1)
    def _():
        o_ref[...] = _layer_norm(alpha * x_ref[...] + acc_ref[...], g2_ref[...], b2_ref[...])


def _moe(x, lw, exp_w, layer, alpha, tm):
    t, d = x.shape
    full = lambda a: pl.BlockSpec(a.shape, lambda i, e: (0,) * a.ndim)
    exp_spec = lambda a: pl.BlockSpec((1, EXPERTS_PER_STEP) + a.shape[2:], lambda i, e: (layer, e, 0, 0))
    small_w = (lw['wr_hi'], lw['wr_lo'], lw['r_bias'], lw['w_sg'], lw['w_su'], lw['w_sd'])
    ln_w = (lw['ln2_g'], lw['ln2_b'])
    return pl.pallas_call(
        functools.partial(_moe_kernel, alpha=alpha),
        grid=(t // tm, N_EXPERTS // EXPERTS_PER_STEP),
        in_specs=[pl.BlockSpec((tm, d), lambda i, e: (i, 0))] + [full(w) for w in small_w]
                 + [exp_spec(w) for w in exp_w] + [full(w) for w in ln_w],
        out_specs=pl.BlockSpec((tm, d), lambda i, e: (i, 0)),
        out_shape=jax.ShapeDtypeStruct((t, d), F32),
        scratch_shapes=[pltpu.VMEM((tm, d), BF16), pltpu.VMEM((tm, N_EXPERTS), F32), pltpu.VMEM((tm, d), F32)],
        compiler_params=_cparams("parallel", "arbitrary"),
        name="moe",
    )(x, *small_w, *exp_w, *ln_w)


def _rope_tables(pos):
    rd = HEAD_DIM // 4
    half = rd // 2
    inv = ROPE_THETA ** (-jnp.arange(half, dtype=F32) * 2.0 / rd)
    ang = pos.astype(F32)[:, None] * inv[None, :]
    cos, sin = jnp.cos(ang), jnp.sin(ang)
    t = pos.shape[0]
    ones = jnp.ones((t, HEAD_DIM - rd), F32)
    zeros = jnp.zeros((t, HEAD_DIM - rd), F32)
    zh = jnp.zeros((t, half), F32)
    c64 = jnp.concatenate([cos, cos, ones], axis=1)
    a64 = jnp.concatenate([-sin, zh, zeros], axis=1)
    b64 = jnp.concatenate([zh, sin, zeros], axis=1)
    rep = lambda a: jnp.tile(a, (1, LANES // HEAD_DIM))
    return rep(c64), rep(a64), rep(b64)


def _split_hi_lo(w):
    hi = w.astype(BF16)
    return hi, (w - hi.astype(F32)).astype(BF16)


def _row_tile(t, want):
    return want if t % want == 0 else t


def kernel(x_prompt, x_sample, cache_k, cache_v, cache_idx_k, state_conv, page_table, w_in, b_in, idx_k_ln_g, idx_k_ln_b, conv_w, conv_b, conv_ln_g, conv_ln_b, w_pa, w_pc, w_o, ln1_g, ln1_b, w_router, router_bias, w_exp_gate, w_exp_up, w_exp_down, w_sh_gate, w_sh_up, w_sh_down, ln2_g, ln2_b):
    bsz, seq, d = x_prompt.shape
    bd, n_new, _ = x_sample.shape
    depth = w_in.shape[0]
    n_pool, page = cache_k.shape[1], cache_k.shape[2]
    past = page_table.shape[1] * page
    alpha = (2 * depth) ** 0.25
    assert seq % KEY_CHUNK == 0 or seq < KEY_CHUNK
    assert page == LANES and page_table.shape[1] % PAGES_PER_STEP == 0 and n_new <= SAMPLE_ROWS

    c_a = 4 * ATTN_W
    c_s = c_a + IDX_DIM + N_IDX_HEADS
    c_g = c_s + 2 * CONV_CH
    pad_s = LANES - (IDX_DIM + N_IDX_HEADS)
    row2 = lambda a: a.reshape(1, -1)

    cache_kt = cache_k.transpose(0, 1, 3, 4, 2)
    cache_vt = cache_v.transpose(0, 1, 3, 4, 2)
    cache_idx_kt = cache_idx_k.transpose(0, 1, 3, 2)
    tables_p = _rope_tables(jnp.arange(seq))
    tables_p = tuple(jnp.tile(a, (bsz, 1)) for a in tables_p)
    tables_s = _rope_tables(past + jnp.arange(n_new))
    tables_s = tuple(jnp.tile(a, (bd, 1)) for a in tables_s)
    state_t = state_conv.transpose(0, 2, 1, 3)

    xp = x_prompt.reshape(bsz * seq, d)
    xs = x_sample.reshape(bd * n_new, d)
    outs = {name: [] for name in ('kp', 'vp', 'kip', 'cp', 'ks', 'vs', 'kis', 'cs')}
    tm_p = _row_tile(bsz * seq, 256)
    tm_s = bd * n_new
    tm_moe = _row_tile(bsz * seq, 1024)
    exp_w = (w_exp_gate, w_exp_up, w_exp_down)

    for l in range(depth):
        wr_hi, wr_lo = _split_hi_lo(w_router[l].T)
        lw = {
            'w_a': w_in[l, :, :c_a].astype(BF16),
            'w_s': jnp.pad(w_in[l, :, c_a:c_s], ((0, 0), (0, pad_s))).astype(BF16),
            'w_g': w_in[l, :, c_s:c_g].astype(BF16),
            'w_t': w_in[l, :, c_g:].astype(BF16),
            'b_a': row2(b_in[l, :c_a]),
            'b_s': row2(jnp.pad(b_in[l, c_a:c_s], (0, pad_s))),
            'b_g': row2(b_in[l, c_s:c_g]),
            'b_t': row2(b_in[l, c_g:]),
            'idx_ln_g': row2(jnp.pad(idx_k_ln_g[l], (0, LANES - IDX_DIM))),
            'idx_ln_b': row2(jnp.pad(idx_k_ln_b[l], (0, LANES - IDX_DIM))),
            'w_pa': w_pa[l].astype(BF16), 'w_pc': w_pc[l].astype(BF16), 'w_o': w_o[l].astype(BF16),
            'conv_ln_g': row2(conv_ln_g[l]), 'conv_ln_b': row2(conv_ln_b[l]),
            'ln1_g': row2(ln1_g[l]), 'ln1_b': row2(ln1_b[l]),
            'wr_hi': wr_hi, 'wr_lo': wr_lo, 'r_bias': router_bias[l].reshape(-1, 1),
            'w_sg': w_sh_gate[l].astype(BF16), 'w_su': w_sh_up[l].astype(BF16), 'w_sd': w_sh_down[l].astype(BF16),
            'ln2_g': row2(ln2_g[l]), 'ln2_b': row2(ln2_b[l]),
        }
        cw = jnp.pad(conv_w[l], ((0, 1), (0, 0)))
        cb = row2(conv_b[l])

        q, kf, vf, vb, qi, small, u, ga, gc, kt, kit = _inproj(xp, lw, tables_p, tm_p)
        r3 = lambda a: a.reshape(bsz, seq, a.shape[-1])
        attn = _prompt_attn(r3(qi), r3(small), r3(q), kit, kt, r3(vb))
        u3 = r3(u)
        conv = _conv_prompt(u3, cw, cb, min(512, seq))
        outs['kp'].append(kf.reshape(bsz, seq, N_HEADS, HEAD_DIM))
        outs['vp'].append(vf.reshape(bsz, seq, N_HEADS, HEAD_DIM))
        outs['kip'].append(r3(small)[:, :, :IDX_DIM])
        outs['cp'].append(u3[:, seq - (CONV_W - 1):])
        x1 = _merge(xp, attn.reshape(bsz * seq, ATTN_W), conv.reshape(bsz * seq, CONV_CH), ga, gc, lw, alpha, tm_p)
        xp = _moe(x1, lw, exp_w, l, alpha, tm_moe)

        q, kf, vf, vb, qi, small, u, ga, gc, kt, kit = _inproj(xs, lw, tables_s, tm_s)
        s3 = lambda a: a.reshape(bd, n_new, a.shape[-1])
        attn = _sample_attention(l, page_table, s3(q), s3(qi), s3(small), kit[:IDX_DIM], kt, s3(vb),
                                 cache_kt, cache_vt, cache_idx_kt)
        u3 = s3(u)
        conv = _conv_sample(state_t[l], u3.transpose(1, 0, 2), cw, cb).transpose(1, 0, 2)
        outs['ks'].append(kf.reshape(bd, n_new, N_HEADS, HEAD_DIM))
        outs['vs'].append(vf.reshape(bd, n_new, N_HEADS, HEAD_DIM))
        outs['kis'].append(s3(small)[:, :, :IDX_DIM])
        outs['cs'].append(jnp.concatenate([state_conv[l][:, n_new:], u3], axis=1))
        x1 = _merge(xs, attn.reshape(bd * n_new, ATTN_W), conv.reshape(bd * n_new, CONV_CH), ga, gc, lw, alpha, tm_s)
        xs = _moe(x1, lw, exp_w, l, alpha, tm_s)

    st = lambda name: jnp.stack(outs[name])
    return (xp.reshape(bsz, seq, d), xs.reshape(bd, n_new, d), st('kp'), st('vp'), st('kip'), st('cp'),
            st('ks'), st('vs'), st('kis'), st('cs'))
```

```python
import functools
import math

import jax
import jax.numpy as jnp
from jax import lax
from jax.experimental import pallas as pl
from jax.experimental.pallas import tpu as pltpu

F32 = jnp.float32
BF16 = jnp.bfloat16
I32 = jnp.int32

N_HEADS = 8
HEAD_DIM = 64
ATTN_W = N_HEADS * HEAD_DIM
N_IDX_HEADS = 8
IDX_DIM = 64
IDX_W_SCALE = (N_IDX_HEADS * IDX_DIM) ** -0.5
TOPK_MAX = 256
ROPE_THETA = 500000.0
CONV_CH = 512
CONV_W = 31
N_EXPERTS = 64
EXPERT_FF = 256
MOE_TOPK = 8
N_GROUPS = 8
GROUP_SIZE = N_EXPERTS // N_GROUPS
TOPK_GROUPS = 4
ROUTED_SCALE = 2.5
LN_EPS = 1e-5

LANES = 128
Q_ROWS = 128
KEY_CHUNK = 512
ATTN_CHUNK = 512
INT_MIN = -(2 ** 31)
BRACKET_SPAN = 1 << 24
NEG_BIG = -1e30
VMEM_LIMIT = 56 * 1024 * 1024


def _cparams(*sem):
    return pltpu.CompilerParams(dimension_semantics=sem, vmem_limit_bytes=VMEM_LIMIT)


def _layer_norm(x, g, b):
    mu = jnp.mean(x, axis=-1, keepdims=True)
    xc = x - mu
    var = jnp.mean(xc * xc, axis=-1, keepdims=True)
    return xc * lax.rsqrt(var + LN_EPS) * g + b


def _dot(a, b):
    return jnp.dot(a, b, preferred_element_type=F32)


def _dot_nt(a, b):
    return lax.dot_general(a, b, (((1,), (1,)), ((), ())), preferred_element_type=F32)


def _inproj_kernel(x_ref, wa_ref, ws_ref, wg_ref, wt_ref, ba_ref, bs_ref, bg_ref, bt_ref,
                   lng_ref, lnb_ref, cos_ref, s1_ref, s2_ref,
                   q_ref, kf_ref, vf_ref, vb_ref, qi_ref, small_ref,
                   u_ref, ga_ref, gc_ref, kt_ref, kit_ref):
    xb = x_ref[...].astype(BF16)
    cos = cos_ref[...]
    s1 = s1_ref[...]
    s2 = s2_ref[...]

    def rope128(v, c, a, b):
        return v * c + pltpu.roll(v, LANES - 8, 1) * a + pltpu.roll(v, 8, 1) * b

    def rope_seg(seg):
        return jnp.concatenate(
            [rope128(seg[:, c * LANES:(c + 1) * LANES], cos, s1, s2) for c in range(ATTN_W // LANES)], axis=1)

    ha = _dot(xb, wa_ref[...]) + ba_ref[...]
    q = rope_seg(ha[:, 0:ATTN_W])
    k = rope_seg(ha[:, ATTN_W:2 * ATTN_W])
    v = ha[:, 2 * ATTN_W:3 * ATTN_W]
    qi = rope_seg(ha[:, 3 * ATTN_W:4 * ATTN_W])
    q_ref[...] = (q * (HEAD_DIM ** -0.5)).astype(BF16)
    kf_ref[...] = k
    kt_ref[...] = k.T.astype(BF16)
    vf_ref[...] = v
    vb_ref[...] = v.astype(BF16)
    qi_ref[...] = qi.astype(BF16)

    hs = _dot(xb, ws_ref[...]) + bs_ref[...]
    lane = lax.broadcasted_iota(I32, hs.shape, 1)
    is_key = lane < IDX_DIM
    mu = jnp.sum(jnp.where(is_key, hs, 0.0), axis=1, keepdims=True) * (1.0 / IDX_DIM)
    hc = jnp.where(is_key, hs - mu, 0.0)
    var = jnp.sum(hc * hc, axis=1, keepdims=True) * (1.0 / IDX_DIM)
    kin = hc * lax.rsqrt(var + LN_EPS) * lng_ref[...] + lnb_ref[...]
    kir = rope128(kin, jnp.where(is_key, cos, 1.0), jnp.where(is_key, s1, 0.0), jnp.where(is_key, s2, 0.0))
    small = jnp.where(is_key, kir, jnp.where(lane < IDX_DIM + N_IDX_HEADS, hs * IDX_W_SCALE, 0.0))
    small_ref[...] = small
    kit = small.T[0:IDX_DIM, :].astype(BF16)
    kit_ref[...] = jnp.concatenate([kit, kit], axis=0)

    hg = _dot(xb, wg_ref[...]) + bg_ref[...]
    u_ref[...] = hg[:, 0:CONV_CH] * jax.nn.sigmoid(hg[:, CONV_CH:2 * CONV_CH])

    d = ga_ref.shape[1]
    ht = jax.nn.sigmoid(_dot(xb, wt_ref[...]) + bt_ref[...])
    ga_ref[...] = ht[:, 0:d]
    gc_ref[...] = ht[:, d:2 * d]


def _inproj(x, lw, tables, tm):
    t, d = x.shape
    cos, s1, s2 = tables
    full = lambda a: pl.BlockSpec(a.shape, lambda i: (0,) * a.ndim)
    rows = lambda w: pl.BlockSpec((tm, w), lambda i: (i, 0))
    weights = (lw['w_a'], lw['w_s'], lw['w_g'], lw['w_t'], lw['b_a'], lw['b_s'], lw['b_g'], lw['b_t'],
               lw['idx_ln_g'], lw['idx_ln_b'])
    out_shapes = (
        jax.ShapeDtypeStruct((t, ATTN_W), BF16),
        jax.ShapeDtypeStruct((t, ATTN_W), F32),
        jax.ShapeDtypeStruct((t, ATTN_W), F32),
        jax.ShapeDtypeStruct((t, ATTN_W), BF16),
        jax.ShapeDtypeStruct((t, ATTN_W), BF16),
        jax.ShapeDtypeStruct((t, LANES), F32),
        jax.ShapeDtypeStruct((t, CONV_CH), F32),
        jax.ShapeDtypeStruct((t, d), F32),
        jax.ShapeDtypeStruct((t, d), F32),
    )
    cols_shapes = (
        jax.ShapeDtypeStruct((ATTN_W, t), BF16),
        jax.ShapeDtypeStruct((LANES, t), BF16),
    )
    cols = lambda h: pl.BlockSpec((h, tm), lambda i: (0, i))
    return pl.pallas_call(
        _inproj_kernel,
        grid=(t // tm,),
        in_specs=[rows(d)] + [full(w) for w in weights] + [rows(LANES)] * 3,
        out_specs=tuple(rows(s.shape[1]) for s in out_shapes) + tuple(cols(s.shape[0]) for s in cols_shapes),
        out_shape=out_shapes + cols_shapes,
        compiler_params=_cparams("parallel"),
        name="inproj",
    )(x, *weights, cos, s1, s2)


def _sort_key(score, admissible):
    bits = pltpu.bitcast(score, I32)
    bits = jnp.where(score == 0.0, 0, bits)
    key = jnp.where(bits >= 0, bits, bits ^ 0x7FFFFFFF)
    return jnp.where(admissible, key, INT_MIN)


def _fold_lanes(x):
    w = x.shape[1]
    acc = x[:, 0:LANES]
    for c in range(1, w // LANES):
        acc = acc + x[:, c * LANES:(c + 1) * LANES]
    return acc


def _select_threshold(keys_ref, rows, chunk, n_chunks, ksel, n_cols, n_adm=None):
    lane = lax.broadcasted_iota(I32, (rows, chunk), 1)

    def count(pred):
        def body(c, cnt):
            off = pl.multiple_of(c * chunk, chunk)
            kc = keys_ref[:, pl.ds(off, chunk)]
            return cnt + _fold_lanes(jnp.where(pred(kc, off + lane), 1, 0))
        cnt = lax.fori_loop(0, n_chunks, body, jnp.zeros((rows, LANES), I32))
        return jnp.sum(cnt.astype(F32), axis=1, keepdims=True).astype(I32)

    def max_body(c, g):
        off = pl.multiple_of(c * chunk, chunk)
        kc = keys_ref[:, pl.ds(off, chunk)]
        for s in range(chunk // LANES):
            g = jnp.maximum(g, kc[:, s * LANES:(s + 1) * LANES])
        return g

    g = lax.fori_loop(0, n_chunks, max_body, jnp.full((rows, LANES), INT_MIN, I32))
    hi0 = jnp.max(g, axis=1, keepdims=True)
    near = jnp.where(hi0 > INT_MIN + BRACKET_SPAN, hi0 - BRACKET_SPAN, INT_MIN)
    lo0 = jnp.where(count(lambda kc, cols: kc >= near) >= ksel, near, INT_MIN)

    if n_adm is None:
        n_adm = count(lambda kc, cols: kc > INT_MIN)
    small = n_adm < ksel
    lo0 = jnp.where(small, INT_MIN, lo0)
    hi0 = jnp.where(small, INT_MIN, hi0)

    width_bits = 32 - lax.clz(hi0 - lo0)
    n_steps = jnp.max(width_bits.astype(F32)).astype(I32)

    def bisect_body(i, state):
        lo, hi = state
        mid = (lo >> 1) + (hi >> 1) + ((lo | hi) & 1)
        ge = count(lambda kc, cols: kc >= mid) >= ksel
        return jnp.where(ge, mid, lo), jnp.where(ge, hi, mid - 1)

    thr, _ = lax.fori_loop(0, n_steps, bisect_body, (lo0, hi0))

    cnt_gt = count(lambda kc, cols: kc > thr)
    cnt_ge = count(lambda kc, cols: kc >= thr)
    need = ksel - cnt_gt
    live = thr != INT_MIN
    over = jnp.logical_and(live, cnt_ge > ksel)
    n_bits = max(1, (n_cols - 1).bit_length())

    def tie_search():
        def body(i, m):
            cand = m | lax.shift_left(jnp.int32(1), n_bits - 1 - i)
            g = count(lambda kc, cols: jnp.logical_and(kc == thr, cols < cand))
            return jnp.where(g < need, cand, m)
        return lax.fori_loop(0, n_bits, body, jnp.zeros((rows, 1), I32))

    any_over = jnp.max(jnp.where(over, 1.0, 0.0)) > 0.5
    m = lax.cond(any_over, tie_search, lambda: jnp.zeros((rows, 1), I32))
    cut = jnp.where(live, jnp.where(over, m, n_cols), -1).astype(I32)
    return thr, cut


def _selected(kc, cols, thr, cut):
    return jnp.logical_or(kc > thr, jnp.logical_and(kc == thr, cols <= cut))


def _head_masked(x):
    lane = lax.broadcasted_iota(I32, (x.shape[0], LANES), 1)
    out = []
    for h in range(N_HEADS):
        pair = x[:, (h // 2) * LANES:(h // 2 + 1) * LANES]
        keep = (lane < HEAD_DIM) if h % 2 == 0 else (lane >= HEAD_DIM)
        out.append(jnp.where(keep, pair, jnp.zeros_like(pair)))
    return out


def _prompt_attn_kernel(qi_ref, small_ref, q_ref, kit_ref, kt_ref, v_ref, o_ref, keys_ref, m_ref, l_ref, acc_ref,
                        *, seq, chunk, achunk, ksel):
    j = pl.program_id(1)
    n_chunks = (j * Q_ROWS + Q_ROWS + chunk - 1) // chunk
    wi = small_ref[0]
    row = j * Q_ROWS + lax.broadcasted_iota(I32, (Q_ROWS, chunk), 0)
    lane = lax.broadcasted_iota(I32, (Q_ROWS, chunk), 1)
    wcols = [wi[:, IDX_DIM + h:IDX_DIM + h + 1] for h in range(N_IDX_HEADS)]
    qi_h = _head_masked(qi_ref[0])

    def idx_body(c, carry):
        off = pl.multiple_of(c * chunk, chunk)
        kic = kit_ref[:, pl.ds(off, chunk)]
        score = jnp.zeros((Q_ROWS, chunk), F32)
        for h in range(N_IDX_HEADS):
            score = score + jnp.maximum(_dot(qi_h[h], kic), 0.0) * wcols[h]
        keys_ref[:, pl.ds(off, chunk)] = _sort_key(score, (off + lane) <= row)
        return carry

    lax.fori_loop(0, n_chunks, idx_body, 0)
    n_adm = jnp.max(row[:, 0:LANES], axis=1, keepdims=True) + 1
    thr, cut = _select_threshold(keys_ref, Q_ROWS, chunk, n_chunks, ksel, seq, n_adm)

    q_h = _head_masked(q_ref[0])
    n_ach = (j * Q_ROWS + Q_ROWS + achunk - 1) // achunk
    lane_a = lax.broadcasted_iota(I32, (Q_ROWS, achunk), 1)
    even = lax.broadcasted_iota(I32, (Q_ROWS, LANES), 1) < HEAD_DIM
    ones = jnp.ones((achunk, LANES), BF16)
    m_ref[...] = jnp.full(m_ref.shape, NEG_BIG, F32)
    l_ref[...] = jnp.zeros_like(l_ref)
    acc_ref[...] = jnp.zeros_like(acc_ref)

    def attn_body(c, carry):
        off = pl.multiple_of(c * achunk, achunk)
        kc = keys_ref[:, pl.ds(off, achunk)]
        bias = jnp.where(_selected(kc, off + lane_a, thr, cut), 0.0, NEG_BIG)
        for pr in range(N_HEADS // 2):
            sl = slice(pr * LANES, (pr + 1) * LANES)
            kk = kt_ref[sl, pl.ds(off, achunk)]
            vv = jnp.concatenate([v_ref[0, pl.ds(off, achunk), sl], ones], axis=1)
            pvs, alphas = [], []
            for h in (2 * pr, 2 * pr + 1):
                hl = slice(h * LANES, (h + 1) * LANES)
                s = _dot(q_h[h], kk) + bias
                m_old = m_ref[:, hl]
                m_new = jnp.maximum(m_old, jnp.max(s, axis=1, keepdims=True))
                m_ref[:, hl] = m_new
                alpha = jnp.exp(m_old - m_new)
                p = jnp.exp(s - jnp.concatenate([m_new] * (achunk // LANES), axis=1))
                pv = _dot(p.astype(BF16), vv)
                l_ref[:, hl] = alpha * l_ref[:, hl] + pv[:, LANES:]
                pvs.append(pv[:, :LANES])
                alphas.append(alpha)
            acc_ref[:, sl] = (jnp.where(even, alphas[0], alphas[1]) * acc_ref[:, sl]
                              + jnp.where(even, pvs[0], pvs[1]))
        return carry

    lax.fori_loop(0, n_ach, attn_body, 0)
    for pr in range(N_HEADS // 2):
        sl = slice(pr * LANES, (pr + 1) * LANES)
        l_pair = jnp.where(even, l_ref[:, 2 * pr * LANES:(2 * pr + 1) * LANES],
                           l_ref[:, (2 * pr + 1) * LANES:(2 * pr + 2) * LANES])
        o_ref[0, :, sl] = (acc_ref[:, sl] / l_pair).astype(o_ref.dtype)


def _prompt_attn(qi, small, q, kit, kt, vb):
    b, s, _ = q.shape
    chunk = min(KEY_CHUNK, s)
    achunk = min(ATTN_CHUNK, s)
    ksel = min(TOPK_MAX, s // 4)
    qblk = lambda w: pl.BlockSpec((1, Q_ROWS, w), lambda bi, j: (bi, j, 0))
    return pl.pallas_call(
        functools.partial(_prompt_attn_kernel, seq=s, chunk=chunk, achunk=achunk, ksel=ksel),
        grid=(b, s // Q_ROWS),
        in_specs=[qblk(ATTN_W), qblk(LANES), qblk(ATTN_W),
                  pl.BlockSpec((LANES, s), lambda bi, j: (0, bi)),
                  pl.BlockSpec((ATTN_W, s), lambda bi, j: (0, bi)),
                  pl.BlockSpec((1, s, ATTN_W), lambda bi, j: (bi, 0, 0))],
        out_specs=qblk(ATTN_W),
        out_shape=jax.ShapeDtypeStruct((b, s, ATTN_W), BF16),
        scratch_shapes=[pltpu.VMEM((Q_ROWS, s), I32), pltpu.VMEM((Q_ROWS, N_HEADS * LANES), F32),
                        pltpu.VMEM((Q_ROWS, N_HEADS * LANES), F32), pltpu.VMEM((Q_ROWS, ATTN_W), F32)],
        compiler_params=_cparams("parallel", "arbitrary"),
        name="prompt_attn",
    )(qi, small, q, kit, kt, vb)


SAMPLE_ROWS = 8
PAGES_PER_STEP = 8
HEAD_ROWS = N_HEADS * SAMPLE_ROWS


def _sample_scores_kernel(pt_ref, qi_ref, wcol_ref, kin_ref, *rest, page, past, n_new):
    pages = rest[:PAGES_PER_STEP]
    keys_ref = rest[PAGES_PER_STEP]
    st = pl.program_id(1)
    n_steps = pl.num_programs(1)
    qi = qi_ref[0]
    wcol = wcol_ref[0]
    lane = lax.broadcasted_iota(I32, (SAMPLE_ROWS, page), 1)
    row = lax.broadcasted_iota(I32, (SAMPLE_ROWS, page), 0)

    def scores(kit_bf16):
        d = jnp.maximum(_dot(qi, kit_bf16), 0.0) * wcol
        sc = d[0:SAMPLE_ROWS]
        for h in range(1, N_IDX_HEADS):
            sc = sc + d[h * SAMPLE_ROWS:(h + 1) * SAMPLE_ROWS]
        return sc

    for p in range(PAGES_PER_STEP):
        off = pl.multiple_of((st * PAGES_PER_STEP + p) * page, page)
        keys_ref[:, pl.ds(off, page)] = _sort_key(scores(pages[p][0, 0].astype(BF16)), row < n_new)

    @pl.when(st == n_steps - 1)
    def _():
        adm = jnp.logical_and(row < n_new, lane <= row)
        keys_ref[:, pl.ds(past, page)] = _sort_key(scores(kin_ref[0]), adm)


def _sample_select_kernel(keys_ref, bias_ref, *, chunk, ksel, n_cols):
    rows = keys_ref.shape[0]
    n_chunks = n_cols // chunk
    thr, cut = _select_threshold(keys_ref, rows, chunk, n_chunks, ksel, n_cols)
    lane = lax.broadcasted_iota(I32, (rows, chunk), 1)

    def body(c, carry):
        off = pl.multiple_of(c * chunk, chunk)
        kc = keys_ref[:, pl.ds(off, chunk)]
        bias_ref[:, pl.ds(off, chunk)] = jnp.where(_selected(kc, off + lane, thr, cut), 0.0, NEG_BIG)
        return carry

    lax.fori_loop(0, n_chunks, body, 0)


def _sample_attn_kernel(pt_ref, q_ref, bias_ref, knew_ref, vnew_ref, *rest, page, past):
    kpages = rest[:PAGES_PER_STEP]
    vpages = rest[PAGES_PER_STEP:2 * PAGES_PER_STEP]
    o_ref = rest[2 * PAGES_PER_STEP]
    m_ref, l_ref, acc_ref = rest[2 * PAGES_PER_STEP + 1:]
    st = pl.program_id(1)
    n_steps = pl.num_programs(1)
    q = q_ref[0]

    @pl.when(st == 0)
    def _():
        m_ref[...] = jnp.full_like(m_ref, NEG_BIG)
        l_ref[...] = jnp.zeros_like(l_ref)
        acc_ref[...] = jnp.zeros_like(acc_ref)

    def scores(kpage_t, bias_tok):
        return _dot(q, kpage_t) + jnp.concatenate([bias_tok] * N_HEADS, axis=0)

    def update(s_list, v_list):
        m_old = m_ref[...]
        m_new = m_old
        for s in s_list:
            m_new = jnp.maximum(m_new, jnp.max(s, axis=1, keepdims=True))
        alpha = jnp.exp(m_old - m_new)
        l_new = alpha * l_ref[...]
        acc = alpha * acc_ref[...]
        for s, vpage_t in zip(s_list, v_list):
            pp = jnp.exp(s - m_new)
            l_new = l_new + jnp.sum(pp, axis=1, keepdims=True)
            acc = acc + _dot_nt(pp.astype(BF16), vpage_t)
        m_ref[...] = m_new
        l_ref[...] = l_new
        acc_ref[...] = acc

    as_rows = lambda ref: ref[0, 0].reshape(ATTN_W, page).astype(BF16)
    s_list = []
    for p in range(PAGES_PER_STEP):
        off = pl.multiple_of((st * PAGES_PER_STEP + p) * page, page)
        s_list.append(scores(as_rows(kpages[p]), bias_ref[0, :, pl.ds(off, page)]))
    update(s_list, [as_rows(vpages[p]) for p in range(PAGES_PER_STEP)])

    @pl.when(st == n_steps - 1)
    def _():
        update([scores(knew_ref[0], bias_ref[0, :, past:past + page])], [vnew_ref[0]])
        out = acc_ref[...] / l_ref[...]
        for h in range(N_HEADS):
            hs = slice(h * HEAD_DIM, (h + 1) * HEAD_DIM)
            o_ref[0, :, hs] = out[h * SAMPLE_ROWS:(h + 1) * SAMPLE_ROWS, hs]


def _sample_attention(layer, page_table, q, qi, small, kit_new, kt_new, vb_new, cache_kt, cache_vt, cache_idx_kt):
    bd, n_new, _ = q.shape
    n_pages = page_table.shape[1]
    page = cache_idx_kt.shape[3]
    past = n_pages * page
    ksel = min(TOPK_MAX, (past + n_new) // 4)
    n_steps = n_pages // PAGES_PER_STEP
    n_cols = past + page

    pad_tok = lambda a: jnp.pad(a, ((0, 0), (0, SAMPLE_ROWS - n_new), (0, 0)))
    head_major = lambda a: pad_tok(a).reshape(bd, SAMPLE_ROWS, N_HEADS, -1).transpose(0, 2, 1, 3)
    qi_rows = head_major(qi).reshape(bd, HEAD_ROWS, IDX_DIM)
    wcol = head_major(small[:, :, IDX_DIM:IDX_DIM + N_IDX_HEADS]).reshape(bd, HEAD_ROWS, 1)
    qh = head_major(q)
    eye = jnp.eye(N_HEADS, dtype=q.dtype)
    q_rows = (qh[:, :, :, None, :] * eye[None, :, None, :, None]).reshape(bd, HEAD_ROWS, ATTN_W)
    new_page = lambda a_t: jnp.pad(a_t.reshape(-1, bd, n_new).transpose(1, 0, 2), ((0, 0), (0, 0), (0, page - n_new)))
    kin_pad, knew_pad = new_page(kit_new), new_page(kt_new)
    vnew_pad = jnp.pad(vb_new.transpose(0, 2, 1), ((0, 0), (0, 0), (0, page - n_new)))

    def kv_page_spec(p):
        return pl.BlockSpec((1, 1, N_HEADS, HEAD_DIM, page),
                            lambda b, st, pt, p=p: (layer, pt[b, st * PAGES_PER_STEP + p], 0, 0, 0))

    def idx_page_spec(p):
        return pl.BlockSpec((1, 1, IDX_DIM, page),
                            lambda b, st, pt, p=p: (layer, pt[b, st * PAGES_PER_STEP + p], 0, 0))

    per_b = lambda shape: pl.BlockSpec((1,) + shape, lambda b, st, pt: (b, 0, 0))

    keys = pl.pallas_call(
        functools.partial(_sample_scores_kernel, page=page, past=past, n_new=n_new),
        grid_spec=pltpu.PrefetchScalarGridSpec(
            num_scalar_prefetch=1,
            grid=(bd, n_steps),
            in_specs=[per_b((HEAD_ROWS, IDX_DIM)), per_b((HEAD_ROWS, 1)), per_b((IDX_DIM, page))]
                     + [idx_page_spec(p) for p in range(PAGES_PER_STEP)],
            out_specs=pl.BlockSpec((SAMPLE_ROWS, n_cols), lambda b, st, pt: (b, 0)),
        ),
        out_shape=jax.ShapeDtypeStruct((bd * SAMPLE_ROWS, n_cols), I32),
        compiler_params=_cparams("parallel", "arbitrary"),
        name="sample_scores",
    )(page_table, qi_rows, wcol, kin_pad, *([cache_idx_kt] * PAGES_PER_STEP))

    sel_rows = min(Q_ROWS, bd * SAMPLE_ROWS)
    bias = pl.pallas_call(
        functools.partial(_sample_select_kernel, chunk=page, ksel=ksel, n_cols=n_cols),
        grid=(bd * SAMPLE_ROWS // sel_rows,),
        in_specs=[pl.BlockSpec((sel_rows, n_cols), lambda i: (i, 0))],
        out_specs=pl.BlockSpec((sel_rows, n_cols), lambda i: (i, 0)),
        out_shape=jax.ShapeDtypeStruct((bd * SAMPLE_ROWS, n_cols), F32),
        compiler_params=_cparams("parallel"),
        name="sample_select",
    )(keys).reshape(bd, SAMPLE_ROWS, n_cols)

    out = pl.pallas_call(
        functools.partial(_sample_attn_kernel, page=page, past=past),
        grid_spec=pltpu.PrefetchScalarGridSpec(
            num_scalar_prefetch=1,
            grid=(bd, n_steps),
            in_specs=[per_b((HEAD_ROWS, ATTN_W)), per_b((SAMPLE_ROWS, n_cols)), per_b((ATTN_W, page)),
                      per_b((ATTN_W, page))] + [kv_page_spec(p) for p in range(PAGES_PER_STEP)] * 2,
            out_specs=per_b((SAMPLE_ROWS, ATTN_W)),
            scratch_shapes=[pltpu.VMEM((HEAD_ROWS, 1), F32), pltpu.VMEM((HEAD_ROWS, 1), F32),
                            pltpu.VMEM((HEAD_ROWS, ATTN_W), F32)],
        ),
        out_shape=jax.ShapeDtypeStruct((bd, SAMPLE_ROWS, ATTN_W), F32),
        compiler_params=_cparams("parallel", "arbitrary"),
        name="sample_attn",
    )(page_table, q_rows, bias, knew_pad, vnew_pad, *([cache_kt] * PAGES_PER_STEP), *([cache_vt] * PAGES_PER_STEP))
    return out[:, :n_new]


CONV_HALO = 32


def _conv_prompt_kernel(u_ref, prev_ref, w_ref, b_ref, o_ref, ext_ref, *, tile):
    i = pl.program_id(1)
    halo = prev_ref[0, tile - CONV_HALO:tile, :]
    ext_ref[0:CONV_HALO, :] = jnp.where(i > 0, halo, 0.0)
    ext_ref[CONV_HALO:CONV_HALO + tile, :] = u_ref[0]
    acc = jnp.broadcast_to(b_ref[...], (tile, CONV_CH))
    base = CONV_HALO - (CONV_W - 1)
    for k in range(CONV_W):
        acc = acc + ext_ref[base + k:base + k + tile, :] * w_ref[k:k + 1, :]
    o_ref[0] = acc


def _conv_prompt(u, w, b, tile):
    bsz, s, c = u.shape
    return pl.pallas_call(
        functools.partial(_conv_prompt_kernel, tile=tile),
        grid=(bsz, s // tile),
        in_specs=[pl.BlockSpec((1, tile, c), lambda bi, i: (bi, i, 0)),
                  pl.BlockSpec((1, tile, c), lambda bi, i: (bi, jnp.maximum(i - 1, 0), 0)),
                  pl.BlockSpec(w.shape, lambda bi, i: (0, 0)),
                  pl.BlockSpec(b.shape, lambda bi, i: (0, 0))],
        out_specs=pl.BlockSpec((1, tile, c), lambda bi, i: (bi, i, 0)),
        out_shape=jax.ShapeDtypeStruct((bsz, s, c), F32),
        scratch_shapes=[pltpu.VMEM((CONV_HALO + tile, c), F32)],
        compiler_params=_cparams("parallel", "arbitrary"),
        name="conv_prompt",
    )(u, u, w, b)


def _conv_sample_kernel(state_ref, u_ref, w_ref, b_ref, o_ref):
    n_hist = state_ref.shape[0]
    n_new = u_ref.shape[0]
    for t in range(n_new):
        acc = jnp.broadcast_to(b_ref[...], o_ref.shape[1:])
        for j in range(t, n_hist):
            acc = acc + state_ref[j] * w_ref[j - t:j - t + 1, :]
        for i in range(t + 1):
            k = n_hist - t + i
            acc = acc + u_ref[i] * w_ref[k:k + 1, :]
        o_ref[t] = acc


def _conv_sample(state_t, u_t, w, b):
    return pl.pallas_call(
        _conv_sample_kernel,
        out_shape=jax.ShapeDtypeStruct(u_t.shape, F32),
        name="conv_sample",
    )(state_t, u_t, w, b)


def _merge_kernel(x_ref, attn_ref, conv_ref, ga_ref, gc_ref, wpa_ref, wpc_ref, wo_ref,
                  cg_ref, cb_ref, g1_ref, b1_ref, o_ref, *, alpha):
    c = _layer_norm(conv_ref[...], cg_ref[...], cb_ref[...])
    c = c * jax.nn.sigmoid(c)
    conv_out = _dot(c.astype(BF16), wpc_ref[...])
    attn_out = _dot(attn_ref[...].astype(BF16), wpa_ref[...])
    mix = ga_ref[...] * attn_out + gc_ref[...] * conv_out
    mix = _dot(mix.astype(BF16), wo_ref[...])
    o_ref[...] = _layer_norm(alpha * x_ref[...] + mix, g1_ref[...], b1_ref[...])


def _merge(x, attn, conv, ga, gc, lw, alpha, tm):
    t, d = x.shape
    rows = lambda a: pl.BlockSpec((tm, a.shape[1]), lambda i: (i, 0))
    full = lambda a: pl.BlockSpec(a.shape, lambda i: (0,) * a.ndim)
    weights = (lw['w_pa'], lw['w_pc'], lw['w_o'], lw['conv_ln_g'], lw['conv_ln_b'], lw['ln1_g'], lw['ln1_b'])
    acts = (x, attn, conv, ga, gc)
    return pl.pallas_call(
        functools.partial(_merge_kernel, alpha=alpha),
        grid=(t // tm,),
        in_specs=[rows(a) for a in acts] + [full(w) for w in weights],
        out_specs=pl.BlockSpec((tm, d), lambda i: (i, 0)),
        out_shape=jax.ShapeDtypeStruct((t, d), F32),
        compiler_params=_cparams("parallel"),
        name="merge",
    )(*acts, *weights)


EXPERTS_PER_STEP = 2


def _route(x, wr_hi_ref, wr_lo_ref, rbias_ref):
    tm = x.shape[0]
    x_hi = x.astype(BF16)
    x_lo = (x - x_hi.astype(F32)).astype(BF16)
    logits = _dot_nt(wr_hi_ref[...], x_hi) + (_dot_nt(wr_hi_ref[...], x_lo) + _dot_nt(wr_lo_ref[...], x_hi))
    scores = jax.nn.sigmoid(logits)
    sel = scores + rbias_ref[...]
    neg_inf = -jnp.inf
    sub8 = lax.broadcasted_iota(I32, (GROUP_SIZE, tm), 0).astype(F32)

    def take_first_max(cur, idx, n):
        m = jnp.max(cur, axis=0, keepdims=True)
        first = jnp.min(jnp.where(cur == m, idx, float(n)), axis=0, keepdims=True)
        return m, idx == first

    gs_rows = []
    for g in range(N_GROUPS):
        blk = sel[g * GROUP_SIZE:(g + 1) * GROUP_SIZE, :]
        m1, hit = take_first_max(blk, sub8, GROUP_SIZE)
        m2 = jnp.max(jnp.where(hit, neg_inf, blk), axis=0, keepdims=True)
        gs_rows.append(m1 + m2)
    gs = jnp.concatenate(gs_rows, axis=0)
    gsel = jnp.zeros((N_GROUPS, tm), jnp.bool_)
    for _ in range(TOPK_GROUPS):
        _, hit = take_first_max(gs, sub8, N_GROUPS)
        gsel = jnp.logical_or(gsel, hit)
        gs = jnp.where(hit, neg_inf, gs)
    gself = jnp.where(gsel, 1.0, 0.0)
    emask = jnp.concatenate(
        [jnp.broadcast_to(gself[g:g + 1, :], (GROUP_SIZE, tm)) for g in range(N_GROUPS)], axis=0) > 0.5
    cur = jnp.where(emask, sel, neg_inf)
    sub64 = lax.broadcasted_iota(I32, (N_EXPERTS, tm), 0).astype(F32)
    chosen = jnp.zeros((N_EXPERTS, tm), jnp.bool_)
    for _ in range(MOE_TOPK):
        _, hit = take_first_max(cur, sub64, N_EXPERTS)
        chosen = jnp.logical_or(chosen, hit)
        cur = jnp.where(hit, neg_inf, cur)
    w = jnp.where(chosen, scores, 0.0)
    w = w / jnp.sum(w, axis=0, keepdims=True) * ROUTED_SCALE
    return w.T


def _moe_kernel(x_ref, wrh_ref, wrl_ref, rb_ref, sg_ref, su_ref, sd_ref, eg_ref, eu_ref, ed_ref,
                g2_ref, b2_ref, o_ref, xb_ref, gate_ref, acc_ref, *, alpha):
    e = pl.program_id(1)

    def ffn(xb, wg, wu):
        hg = _dot(xb, wg)
        return hg * jax.nn.sigmoid(hg) * _dot(xb, wu)

    @pl.when(e == 0)
    def _():
        x = x_ref[...]
        xb = x.astype(BF16)
        xb_ref[...] = xb
        gate_ref[...] = _route(x, wrh_ref, wrl_ref, rb_ref)
        acc_ref[...] = _dot(ffn(xb, sg_ref[...], su_ref[...]).astype(BF16), sd_ref[...])

    xb = xb_ref[...]
    gate = gate_ref[...]
    lane = lax.broadcasted_iota(I32, gate.shape, 1)
    hs = []
    for i in range(EXPERTS_PER_STEP):
        gcol = jnp.sum(jnp.where(lane == e * EXPERTS_PER_STEP + i, gate, 0.0), axis=1, keepdims=True)
        hs.append((ffn(xb, eg_ref[0, i].astype(BF16), eu_ref[0, i].astype(BF16)) * gcol).astype(BF16))
    wd = jnp.concatenate([ed_ref[0, i].astype(BF16) for i in range(EXPERTS_PER_STEP)], axis=0)
    acc_ref[...] += _dot(jnp.concatenate(hs, axis=1), wd)

    @pl.when(e == pl.num_programs(1) - 1)
    def _():
        o_ref[...] = _layer_norm(alpha * x_ref[...] + acc_ref[...], g2_ref[...], b2_ref[...])


def _moe(x, lw, exp_w, layer, alpha, tm):
    t, d = x.shape
    full = lambda a: pl.BlockSpec(a.shape, lambda i, e: (0,) * a.ndim)
    exp_spec = lambda a: pl.BlockSpec((1, EXPERTS_PER_STEP) + a.shape[2:], lambda i, e: (layer, e, 0, 0))
    small_w = (lw['wr_hi'], lw['wr_lo'], lw['r_bias'], lw['w_sg'], lw['w_su'], lw['w_sd'])
    ln_w = (lw['ln2_g'], lw['ln2_b'])
    return pl.pallas_call(
        functools.partial(_moe_kernel, alpha=alpha),
        grid=(t // tm, N_EXPERTS // EXPERTS_PER_STEP),
        in_specs=[pl.BlockSpec((tm, d), lambda i, e: (i, 0))] + [full(w) for w in small_w]
                 + [exp_spec(w) for w in exp_w] + [full(w) for w in ln_w],
        out_specs=pl.BlockSpec((tm, d), lambda i, e: (i, 0)),
        out_shape=jax.ShapeDtypeStruct((t, d), F32),
        scratch_shapes=[pltpu.VMEM((tm, d), BF16), pltpu.VMEM((tm, N_EXPERTS), F32), pltpu.VMEM((tm, d), F32)],
        compiler_params=_cparams("parallel", "arbitrary"),
        name="moe",
    )(x, *small_w, *exp_w, *ln_w)


def _rope_tables(pos):
    rd = HEAD_DIM // 4
    half = rd // 2
    inv = ROPE_THETA ** (-jnp.arange(half, dtype=F32) * 2.0 / rd)
    ang = pos.astype(F32)[:, None] * inv[None, :]
    cos, sin = jnp.cos(ang), jnp.sin(ang)
    t = pos.shape[0]
    ones = jnp.ones((t, HEAD_DIM - rd), F32)
    zeros = jnp.zeros((t, HEAD_DIM - rd), F32)
    zh = jnp.zeros((t, half), F32)
    c64 = jnp.concatenate([cos, cos, ones], axis=1)
    a64 = jnp.concatenate([-sin, zh, zeros], axis=1)
    b64 = jnp.concatenate([zh, sin, zeros], axis=1)
    rep = lambda a: jnp.tile(a, (1, LANES // HEAD_DIM))
    return rep(c64), rep(a64), rep(b64)


def _split_hi_lo(w):
    hi = w.astype(BF16)
    return hi, (w - hi.astype(F32)).astype(BF16)


def _row_tile(t, want):
    return want if t % want == 0 else t


def kernel(x_prompt, x_sample, cache_k, cache_v, cache_idx_k, state_conv, page_table, w_in, b_in, idx_k_ln_g, idx_k_ln_b, conv_w, conv_b, conv_ln_g, conv_ln_b, w_pa, w_pc, w_o, ln1_g, ln1_b, w_router, router_bias, w_exp_gate, w_exp_up, w_exp_down, w_sh_gate, w_sh_up, w_sh_down, ln2_g, ln2_b):
    bsz, seq, d = x_prompt.shape
    bd, n_new, _ = x_sample.shape
    depth = w_in.shape[0]
    n_pool, page = cache_k.shape[1], cache_k.shape[2]
    past = page_table.shape[1] * page
    alpha = (2 * depth) ** 0.25
    assert seq % KEY_CHUNK == 0 or seq < KEY_CHUNK
    assert page == LANES and page_table.shape[1] % PAGES_PER_STEP == 0 and n_new <= SAMPLE_ROWS

    c_a = 4 * ATTN_W
    c_s = c_a + IDX_DIM + N_IDX_HEADS
    c_g = c_s + 2 * CONV_CH
    pad_s = LANES - (IDX_DIM + N_IDX_HEADS)
    row2 = lambda a: a.reshape(1, -1)

    cache_kt = cache_k.transpose(0, 1, 3, 4, 2)
    cache_vt = cache_v.transpose(0, 1, 3, 4, 2)
    cache_idx_kt = cache_idx_k.transpose(0, 1, 3, 2)
    tables_p = _rope_tables(jnp.arange(seq))
    tables_p = tuple(jnp.tile(a, (bsz, 1)) for a in tables_p)
    tables_s = _rope_tables(past + jnp.arange(n_new))
    tables_s = tuple(jnp.tile(a, (bd, 1)) for a in tables_s)
    state_t = state_conv.transpose(0, 2, 1, 3)

    xp = x_prompt.reshape(bsz * seq, d)
    xs = x_sample.reshape(bd * n_new, d)
    outs = {name: [] for name in ('kp', 'vp', 'kip', 'cp', 'ks', 'vs', 'kis', 'cs')}
    tm_p = _row_tile(bsz * seq, 256)
    tm_s = bd * n_new
    tm_moe = _row_tile(bsz * seq, 1024)
    exp_w = (w_exp_gate, w_exp_up, w_exp_down)

    for l in range(depth):
        wr_hi, wr_lo = _split_hi_lo(w_router[l].T)
        lw = {
            'w_a': w_in[l, :, :c_a].astype(BF16),
            'w_s': jnp.pad(w_in[l, :, c_a:c_s], ((0, 0), (0, pad_s))).astype(BF16),
            'w_g': w_in[l, :, c_s:c_g].astype(BF16),
            'w_t': w_in[l, :, c_g:].astype(BF16),
            'b_a': row2(b_in[l, :c_a]),
            'b_s': row2(jnp.pad(b_in[l, c_a:c_s], (0, pad_s))),
            'b_g': row2(b_in[l, c_s:c_g]),
            'b_t': row2(b_in[l, c_g:]),
            'idx_ln_g': row2(jnp.pad(idx_k_ln_g[l], (0, LANES - IDX_DIM))),
            'idx_ln_b': row2(jnp.pad(idx_k_ln_b[l], (0, LANES - IDX_DIM))),
            'w_pa': w_pa[l].astype(BF16), 'w_pc': w_pc[l].astype(BF16), 'w_o': w_o[l].astype(BF16),
            'conv_ln_g': row2(conv_ln_g[l]), 'conv_ln_b': row2(conv_ln_b[l]),
            'ln1_g': row2(ln1_g[l]), 'ln1_b': row2(ln1_b[l]),
            'wr_hi': wr_hi, 'wr_lo': wr_lo, 'r_bias': router_bias[l].reshape(-1, 1),
            'w_sg': w_sh_gate[l].astype(BF16), 'w_su': w_sh_up[l].astype(BF16), 'w_sd': w_sh_down[l].astype(BF16),
            'ln2_g': row2(ln2_g[l]), 'ln2_b': row2(ln2_b[l]),
        }
        cw = jnp.pad(conv_w[l], ((0, 1), (0, 0)))
        cb = row2(conv_b[l])

        q, kf, vf, vb, qi, small, u, ga, gc, kt, kit = _inproj(xp, lw, tables_p, tm_p)
        r3 = lambda a: a.reshape(bsz, seq, a.shape[-1])
        attn = _prompt_attn(r3(qi), r3(small), r3(q), kit, kt, r3(vb))
        u3 = r3(u)
        conv = _conv_prompt(u3, cw, cb, min(512, seq))
        outs['kp'].append(kf.reshape(bsz, seq, N_HEADS, HEAD_DIM))
        outs['vp'].append(vf.reshape(bsz, seq, N_HEADS, HEAD_DIM))
        outs['kip'].append(r3(small)[:, :, :IDX_DIM])
        outs['cp'].append(u3[:, seq - (CONV_W - 1):])
        x1 = _merge(xp, attn.reshape(bsz * seq, ATTN_W), conv.reshape(bsz * seq, CONV_CH), ga, gc, lw, alpha, tm_p)
        xp = _moe(x1, lw, exp_w, l, alpha, tm_moe)

        q, kf, vf, vb, qi, small, u, ga, gc, kt, kit = _inproj(xs, lw, tables_s, tm_s)
        s3 = lambda a: a.reshape(bd, n_new, a.shape[-1])
        attn = _sample_attention(l, page_table, s3(q), s3(qi), s3(small), kit[:IDX_DIM], kt, s3(vb),
                                 cache_kt, cache_vt, cache_idx_kt)
        u3 = s3(u)
        conv = _conv_sample(state_t[l], u3.transpose(1, 0, 2), cw, cb).transpose(1, 0, 2)
        outs['ks'].append(kf.reshape(bd, n_new, N_HEADS, HEAD_DIM))
        outs['vs'].append(vf.reshape(bd, n_new, N_HEADS, HEAD_DIM))
        outs['kis'].append(s3(small)[:, :, :IDX_DIM])
        outs['cs'].append(jnp.concatenate([state_conv[l][:, n_new:], u3], axis=1))
        x1 = _merge(xs, attn.reshape(bd * n_new, ATTN_W), conv.reshape(bd * n_new, CONV_CH), ga, gc, lw, alpha, tm_s)
        xs = _moe(x1, lw, exp_w, l, alpha, tm_s)

    st = lambda name: jnp.stack(outs[name])
    return (xp.reshape(bsz, seq, d), xs.reshape(bd, n_new, d), st('kp'), st('vp'), st('kip'), st('cp'),
            st('ks'), st('vs'), st('kis'), st('cs'))
```
